```python
import jax, jax.numpy as jnp
from jax import lax
import numpy as np

D_MODEL = 2048
BATCH = 4
SEQ = 2048
DEPTH = 4
DEC_BATCH = 8
DEC_SEQ = 8
PAST_LEN = 16384
PAGE_SIZE = 128

N_MIXERS = 3
N_SB_LAYERS = len(range(0, DEPTH, N_MIXERS))
N_GLA_LAYERS = len(range(1, DEPTH, N_MIXERS))
N_CONV_LAYERS = len(range(2, DEPTH, N_MIXERS))
SB_HEADS = 16
SB_HEAD_DIM = D_MODEL // SB_HEADS
SB_WIDTH = SB_HEADS * SB_HEAD_DIM
SB_BIAS_HI = -4.0
SB_BIAS_LO = -8.0
Q_BLOCK = 128
GLA_HEADS = 4
GLA_DK = D_MODEL // 2
GLA_DV = D_MODEL
GLA_DKH = GLA_DK // GLA_HEADS
GLA_DVH = GLA_DV // GLA_HEADS
GLA_RANK = 16
GLA_TAU = 16.0
GLA_CHUNK = 64
GLA_IN = 2 * GLA_DK + 2 * GLA_DV + GLA_RANK
CONV_WIDTH = 3
FFN_HIDDEN = 5632
N_MOD = 9
RMS_EPS = 1e-6

kernel_name = "hybrid_sb_gla_conv_macaron_adaln_step"


def rmsnorm(x, g):
    xf = x.astype(jnp.float32)
    y = xf * lax.rsqrt(jnp.mean(xf * xf, axis=-1, keepdims=True) + RMS_EPS)
    return (y * g.astype(jnp.float32)).astype(x.dtype)


def modulation(c, w, b):
    m = jax.nn.silu(c) @ w + b
    return m.reshape(c.shape[0], N_MOD, 1, -1)


def pre_norm(x, g, m, i):
    return rmsnorm(x, g) * (1.0 + m[:, 3 * i + 1]) + m[:, 3 * i]


def swiglu(h, w_in, w_out):
    a, b = jnp.split(h @ w_in, 2, axis=-1)
    return (jax.nn.silu(a) * b) @ w_out


def ffn_half(x, m, i, g, w_in, w_out):
    return x + 0.5 * m[:, 3 * i + 2] * swiglu(pre_norm(x, g, m, i), w_in, w_out)


def sb_attend(q, k, v, q_pos, k_pos, bias):
    z = jnp.einsum("bqhd,bkhd->bhqk", q, k, preferred_element_type=jnp.float32) * (SB_HEAD_DIM ** -0.5)
    z = z + bias.astype(jnp.float32)[None, :, None, None]
    causal = k_pos[None, :] < q_pos[:, None]
    log_not = jnp.where(causal, jax.nn.log_sigmoid(-z), 0.0)
    between = lax.cumsum(log_not, axis=3, reverse=True) - log_not
    a = jnp.where(causal, jnp.exp(jax.nn.log_sigmoid(z) + between), 0.0)
    return jnp.einsum("bhqk,bkhd->bqhd", a.astype(v.dtype), v)


def sb_sweep(q, k, v, q_pos, k_pos, bias):
    b, tq, h, d = q.shape
    blk = min(Q_BLOCK, tq)
    nb = tq // blk
    qb = q.reshape(b, nb, blk, h, d).swapaxes(0, 1)
    pb = q_pos.reshape(nb, blk)
    out = lax.map(lambda a: sb_attend(a[0], k, v, a[1], k_pos, bias), (qb, pb))
    return out.swapaxes(0, 1).reshape(b, tq, h, d)


def sb_mix(h, k_past, v_past, q_pos, k_pos, w_in, w_out, bias):
    b, t, _ = h.shape
    q, k, v = jnp.split(h @ w_in, 3, axis=-1)
    q = q.reshape(b, t, SB_HEADS, SB_HEAD_DIM)
    k = k.reshape(b, t, SB_HEADS, SB_HEAD_DIM)
    v = v.reshape(b, t, SB_HEADS, SB_HEAD_DIM)
    k_all = k if k_past is None else jnp.concatenate([k_past.astype(k.dtype), k], axis=1)
    v_all = v if v_past is None else jnp.concatenate([v_past.astype(v.dtype), v], axis=1)
    o = sb_sweep(q, k_all, v_all, q_pos, k_pos, bias).reshape(b, t, SB_WIDTH)
    return o @ w_out, k, v


def gla_chunk(S, inp):
    q, k, v, g = inp
    q = q.astype(jnp.float32)
    k = k.astype(jnp.float32)
    v = v.astype(jnp.float32)
    c = q.shape[1]
    bcum = jnp.cumsum(g, axis=1)
    inter = jnp.einsum("bthk,bhkv->bthv", q * jnp.exp(bcum), S)
    diff = bcum[:, :, None] - bcum[:, None, :]
    mask = jnp.tril(jnp.ones((c, c), dtype=bool))[None, :, :, None, None]
    decay = jnp.exp(jnp.where(mask, diff, -jnp.inf))
    scores = jnp.einsum("bthk,bshk,btshk->bhts", q, k, decay)
    intra = jnp.einsum("bhts,bshv->bthv", scores, v)
    b_last = bcum[:, -1]
    S_new = jnp.exp(b_last)[..., None] * S + jnp.einsum("bshk,bshv->bhkv", k * jnp.exp(b_last[:, None] - bcum), v)
    return S_new, inter + intra


def gla_mix(h, S0, w_in, w_gate, b_gate, norm_g, w_out):
    b, t, _ = h.shape
    q, k, v, r, glr = jnp.split(h @ w_in, [GLA_DK, 2 * GLA_DK, 2 * GLA_DK + GLA_DV, 2 * GLA_DK + 2 * GLA_DV], axis=-1)
    q = q.reshape(b, t, GLA_HEADS, GLA_DKH) * (GLA_DKH ** -0.5)
    k = k.reshape(b, t, GLA_HEADS, GLA_DKH)
    v = v.reshape(b, t, GLA_HEADS, GLA_DVH)
    g = (jax.nn.log_sigmoid((glr @ w_gate + b_gate).astype(jnp.float32)) / GLA_TAU).reshape(b, t, GLA_HEADS, GLA_DKH)
    c = min(GLA_CHUNK, t)
    nc = t // c
    to_chunks = lambda a: a.reshape(b, nc, c, *a.shape[2:]).swapaxes(0, 1)
    S, o = lax.scan(gla_chunk, S0.astype(jnp.float32), tuple(to_chunks(a) for a in (q, k, v, g)))
    o = o.swapaxes(0, 1).reshape(b, t, GLA_HEADS, GLA_DVH)
    o = rmsnorm(o, norm_g).reshape(b, t, GLA_DV).astype(h.dtype) * jax.nn.silu(r)
    return o @ w_out, S


def conv_mix(h, buf, w_in, w_conv, w_out):
    t = h.shape[1]
    gb, gc, u = jnp.split(h @ w_in, 3, axis=-1)
    z = jnp.concatenate([buf.astype(h.dtype), gc * u], axis=1)
    conv = sum(w_conv[j] * z[:, j:j + t] for j in range(CONV_WIDTH))
    return (gb * conv) @ w_out, z[:, t:]


def setup_inputs(seed: int = 0) -> dict:
    key = jax.random.key(seed)
    ks = jax.random.split(key, 26)
    n_pages = PAST_LEN // PAGE_SIZE
    n_pool = (DEC_BATCH * n_pages * 5) // 4
    D = D_MODEL

    def nrm(k, shape, s=1.0):
        return jax.random.normal(k, shape, jnp.float32) * s

    page_table = jax.random.permutation(ks[6], n_pool)[: DEC_BATCH * n_pages].reshape(DEC_BATCH, n_pages).astype(jnp.int32)
    sb_bias = jnp.linspace(SB_BIAS_HI, SB_BIAS_LO, SB_HEADS, dtype=jnp.float32)[None, :] + nrm(ks[25], (N_SB_LAYERS, SB_HEADS), 0.1)
    return {
        "x_prompt": nrm(ks[0], (BATCH, SEQ, D)),
        "x_sample": nrm(ks[1], (DEC_BATCH, DEC_SEQ, D)),
        "c_prompt": nrm(ks[2], (BATCH, D)),
        "c_sample": nrm(ks[3], (DEC_BATCH, D)),
        "cache_sb_k": nrm(ks[4], (N_SB_LAYERS, n_pool, PAGE_SIZE, SB_HEADS, SB_HEAD_DIM)),
        "cache_sb_v": nrm(ks[5], (N_SB_LAYERS, n_pool, PAGE_SIZE, SB_HEADS, SB_HEAD_DIM)),
        "page_table": page_table,
        "state_gla": nrm(ks[7], (N_GLA_LAYERS, DEC_BATCH, GLA_HEADS, GLA_DKH, GLA_DVH)),
        "state_conv": nrm(ks[8], (N_CONV_LAYERS, DEC_BATCH, CONV_WIDTH - 1, D)),
        "norm_g": 1.0 + nrm(ks[9], (DEPTH, 3, D), 0.02),
        "w_mod": nrm(ks[10], (DEPTH, D, N_MOD * D), 0.5 * D ** -0.5),
        "b_mod": nrm(ks[11], (DEPTH, N_MOD * D), 0.02),
        "ffn_w_in": nrm(ks[12], (DEPTH, 2, D, 2 * FFN_HIDDEN), D ** -0.5),
        "ffn_w_out": nrm(ks[13], (DEPTH, 2, FFN_HIDDEN, D), FFN_HIDDEN ** -0.5),
        "sb_w_in": nrm(ks[14], (N_SB_LAYERS, D, 3 * SB_WIDTH), D ** -0.5),
        "sb_w_out": nrm(ks[15], (N_SB_LAYERS, SB_WIDTH, D), SB_WIDTH ** -0.5),
        "sb_bias": sb_bias,
        "gla_w_in": nrm(ks[16], (N_GLA_LAYERS, D, GLA_IN), D ** -0.5),
        "gla_w_gate": nrm(ks[17], (N_GLA_LAYERS, GLA_RANK, GLA_DK), GLA_RANK ** -0.5),
        "gla_b_gate": nrm(ks[18], (N_GLA_LAYERS, GLA_DK), 0.02),
        "gla_norm_g": 1.0 + nrm(ks[19], (N_GLA_LAYERS, GLA_DVH), 0.02),
        "gla_w_out": nrm(ks[20], (N_GLA_LAYERS, GLA_DV, D), GLA_DV ** -0.5),
        "conv_w_in": nrm(ks[21], (N_CONV_LAYERS, D, 3 * D), D ** -0.5),
        "conv_w": nrm(ks[22], (N_CONV_LAYERS, CONV_WIDTH, D), CONV_WIDTH ** -0.5),
        "conv_w_out": nrm(ks[23], (N_CONV_LAYERS, D, D), D ** -0.5),
        "final_norm_g": 1.0 + nrm(ks[24], (D,), 0.02),
    }


def reference(x_prompt, x_sample, c_prompt, c_sample, cache_sb_k, cache_sb_v, page_table, state_gla, state_conv,
              norm_g, w_mod, b_mod, ffn_w_in, ffn_w_out, sb_w_in, sb_w_out, sb_bias, gla_w_in, gla_w_gate, gla_b_gate,
              gla_norm_g, gla_w_out, conv_w_in, conv_w, conv_w_out, final_norm_g):
    batch, seq, _ = x_prompt.shape
    dec_b, dec_seq, _ = x_sample.shape
    n_pages = page_table.shape[1]
    past_len = n_pages * cache_sb_k.shape[2]
    pos_prompt = jnp.arange(seq)
    pos_sample_q = past_len + jnp.arange(dec_seq)
    pos_sample_k = jnp.arange(past_len + dec_seq)

    xp, xs = x_prompt, x_sample
    sbk_p, sbv_p, sbk_s, sbv_s = [], [], [], []
    gla_p, gla_s, conv_p, conv_s = [], [], [], []

    for l in range(DEPTH):
        mp = modulation(c_prompt, w_mod[l], b_mod[l])
        ms = modulation(c_sample, w_mod[l], b_mod[l])
        xp = ffn_half(xp, mp, 0, norm_g[l, 0], ffn_w_in[l, 0], ffn_w_out[l, 0])
        xs = ffn_half(xs, ms, 0, norm_g[l, 0], ffn_w_in[l, 0], ffn_w_out[l, 0])
        hp = pre_norm(xp, norm_g[l, 1], mp, 1)
        hs = pre_norm(xs, norm_g[l, 1], ms, 1)
        kind, j = l % N_MIXERS, l // N_MIXERS
        if kind == 0:
            yp, kp, vp = sb_mix(hp, None, None, pos_prompt, pos_prompt, sb_w_in[j], sb_w_out[j], sb_bias[j])
            past_k = cache_sb_k[j][page_table].reshape(dec_b, past_len, SB_HEADS, SB_HEAD_DIM)
            past_v = cache_sb_v[j][page_table].reshape(dec_b, past_len, SB_HEADS, SB_HEAD_DIM)
            ys, ks_, vs_ = sb_mix(hs, past_k, past_v, pos_sample_q, pos_sample_k, sb_w_in[j], sb_w_out[j], sb_bias[j])
            sbk_p.append(kp)
            sbv_p.append(vp)
            sbk_s.append(ks_)
            sbv_s.append(vs_)
        elif kind == 1:
            s0 = jnp.zeros((batch, GLA_HEADS, GLA_DKH, GLA_DVH), jnp.float32)
            yp, sp = gla_mix(hp, s0, gla_w_in[j], gla_w_gate[j], gla_b_gate[j], gla_norm_g[j], gla_w_out[j])
            ys, ss = gla_mix(hs, state_gla[j], gla_w_in[j], gla_w_gate[j], gla_b_gate[j], gla_norm_g[j], gla_w_out[j])
            gla_p.append(sp)
            gla_s.append(ss)
        else:
            b0 = jnp.zeros((batch, CONV_WIDTH - 1, hp.shape[-1]), hp.dtype)
            yp, bp = conv_mix(hp, b0, conv_w_in[j], conv_w[j], conv_w_out[j])
            ys, bs = conv_mix(hs, state_conv[j], conv_w_in[j], conv_w[j], conv_w_out[j])
            conv_p.append(bp)
            conv_s.append(bs)
        xp = xp + mp[:, 5] * yp
        xs = xs + ms[:, 5] * ys
        xp = ffn_half(xp, mp, 2, norm_g[l, 2], ffn_w_in[l, 1], ffn_w_out[l, 1])
        xs = ffn_half(xs, ms, 2, norm_g[l, 2], ffn_w_in[l, 1], ffn_w_out[l, 1])

    y_prompt = rmsnorm(xp, final_norm_g)
    y_sample = rmsnorm(xs, final_norm_g)
    return (y_prompt, y_sample, jnp.stack(sbk_p), jnp.stack(sbv_p), jnp.stack(sbk_s), jnp.stack(sbv_s),
            jnp.stack(gla_p), jnp.stack(gla_s), jnp.stack(conv_p), jnp.stack(conv_s))
```

```python
import functools

import jax
import jax.numpy as jnp
from jax import lax
from jax.experimental import pallas as pl
from jax.experimental.pallas import tpu as pltpu

F32 = jnp.float32
BF16 = jnp.bfloat16

N_MIXERS = 3
N_MOD = 9
SB_HEADS = 16
GLA_HEADS = 4
GLA_RANK = 16
GLA_TAU = 16.0
GLA_CHUNK = 64
CONV_WIDTH = 3
RMS_EPS = 1e-6

LANES = 128
SUBLANES = 8
VMEM_BYTES_V7X = 64 * 1024 * 1024
VMEM_LIMIT = VMEM_BYTES_V7X - 8 * 1024 * 1024

ROW_TILE = 1024
COL_TILE = 1024
FFN_COL_TILE = 512
OUT_COL_TILE = 512
OUT_ROW_TILE = 512
NORM_ROW_TILE = 512
MOD_COL_TILE = 1024
SB_Q_TILE = 256
SB_K_TILE = 256
PAGES_PER_STEP = 4


def _params(*sem):
    return pltpu.CompilerParams(dimension_semantics=sem, vmem_limit_bytes=VMEM_LIMIT)


def _sigmoid(x):
    return 1.0 / (1.0 + jnp.exp(-x))


def _softplus(z):
    return jnp.maximum(z, 0.0) + jnp.log1p(jnp.exp(-jnp.abs(z)))


def _split_bf16(x):
    hi = x.astype(BF16)
    lo = (x - hi.astype(F32)).astype(BF16)
    return hi, lo


def _dot(a, b):
    return jnp.dot(a, b, preferred_element_type=F32)


def _dot_nt(a, b):
    return lax.dot_general(a, b, (((1,), (1,)), ((), ())), preferred_element_type=F32)


def _dot_tn(a, b):
    return lax.dot_general(a, b, (((0,), (0,)), ((), ())), preferred_element_type=F32)


def _mod_kernel(c_ref, w_ref, b_ref, o_ref):
    c = c_ref[...]
    s = (c * _sigmoid(c)).astype(BF16)
    o_ref[...] = _dot(s, w_ref[...].astype(BF16)) + b_ref[...]


def _modulation(c_all, w_mod, b_mod):
    depth, d, n = w_mod.shape
    rows = c_all.shape[0]
    tn = MOD_COL_TILE
    return pl.pallas_call(
        _mod_kernel,
        grid=(depth, n // tn),
        in_specs=[
            pl.BlockSpec((rows, d), lambda l, j: (0, 0)),
            pl.BlockSpec((None, d, tn), lambda l, j: (l, 0, j)),
            pl.BlockSpec((None, 1, tn), lambda l, j: (l, 0, j)),
        ],
        out_specs=pl.BlockSpec((None, rows, tn), lambda l, j: (l, 0, j)),
        out_shape=jax.ShapeDtypeStruct((depth, rows, n), F32),
        compiler_params=_params("arbitrary", "arbitrary"),
        name="modulation",
    )(c_all, w_mod, b_mod.reshape(depth, 1, n))


class _Stream:
    def __init__(self, batch, seq, mod, row_tile, out_row_tile, norm_row_tile):
        self.batch, self.seq, self.mod = batch, seq, mod
        self.rows = batch * seq
        self.row_tile, self.out_row_tile, self.norm_row_tile = row_tile, out_row_tile, norm_row_tile
        self.per_tile_mod = mod.shape[3] != 1

    def mod_spec(self, layer, chunk, tm, width, col_of, row_of):
        r = self.mod.shape[3]
        if self.per_tile_mod:
            assert r == tm
            group = lambda *g: 0
        else:
            tiles_per_seq = self.seq // tm
            group = lambda *g: row_of(*g) // tiles_per_seq
        return pl.BlockSpec((None, None, None, r, width),
                            lambda *g: (layer, group(*g), chunk, 0, col_of(*g)))


def _norm_mod_kernel(x_ref, g_ref, shift_ref, scale_ref, o_ref):
    x = x_ref[...]
    y = x * lax.rsqrt(jnp.mean(x * x, axis=-1, keepdims=True) + RMS_EPS) * g_ref[...]
    o_ref[...] = (y * (1.0 + scale_ref[...]) + shift_ref[...]).astype(o_ref.dtype)


def _norm_mod(st, x, g, layer, sub):
    m, d = x.shape
    tm = st.norm_row_tile
    row_of = lambda i: i
    col_of = lambda i: 0
    return pl.pallas_call(
        _norm_mod_kernel,
        grid=(m // tm,),
        in_specs=[
            pl.BlockSpec((tm, d), lambda i: (i, 0)),
            pl.BlockSpec((1, d), lambda i: (0, 0)),
            st.mod_spec(layer, 3 * sub, tm, d, col_of, row_of),
            st.mod_spec(layer, 3 * sub + 1, tm, d, col_of, row_of),
        ],
        out_specs=pl.BlockSpec((tm, d), lambda i: (i, 0)),
        out_shape=jax.ShapeDtypeStruct((m, d), BF16),
        compiler_params=_params("arbitrary"),
        name="norm_mod",
    )(x, g.reshape(1, d), st.mod, st.mod)


def _final_norm_kernel(x_ref, g_ref, o_ref):
    x = x_ref[...]
    o_ref[...] = x * lax.rsqrt(jnp.mean(x * x, axis=-1, keepdims=True) + RMS_EPS) * g_ref[...]


def _final_norm(st, x, g):
    m, d = x.shape
    tm = st.norm_row_tile
    return pl.pallas_call(
        _final_norm_kernel,
        grid=(m // tm,),
        in_specs=[pl.BlockSpec((tm, d), lambda i: (i, 0)), pl.BlockSpec((1, d), lambda i: (0, 0))],
        out_specs=pl.BlockSpec((tm, d), lambda i: (i, 0)),
        out_shape=jax.ShapeDtypeStruct((m, d), F32),
        compiler_params=_params("arbitrary"),
        name="final_norm",
    )(x, g.reshape(1, d))


def _mm_kernel(a_ref, w_ref, o_ref, wb_ref):
    @pl.when(pl.program_id(1) == 0)
    def _():
        wb_ref[...] = w_ref[...].astype(BF16)

    o_ref[...] = _dot(a_ref[...], wb_ref[...]).astype(o_ref.dtype)


def _in_proj(st, a, w, widx, n_parts, out_dtype=F32):
    m, k = a.shape
    tm, tn = st.row_tile, COL_TILE
    tiles_per_part = k // tn
    nlead = len(widx)
    return pl.pallas_call(
        _mm_kernel,
        grid=(n_parts * tiles_per_part, m // tm),
        in_specs=[
            pl.BlockSpec((tm, k), lambda j, i: (i, 0)),
            pl.BlockSpec((None,) * nlead + (k, tn), lambda j, i: widx + (0, j)),
        ],
        out_specs=pl.BlockSpec((None, tm, tn), lambda j, i: (j // tiles_per_part, i, j % tiles_per_part)),
        out_shape=jax.ShapeDtypeStruct((n_parts, m, k), out_dtype),
        scratch_shapes=[pltpu.VMEM((k, tn), BF16)],
        compiler_params=_params("arbitrary", "arbitrary"),
        name="in_proj",
    )(a, w)


def _ffn_in_kernel(a_ref, wa_ref, wb_ref, o_ref, wab_ref, wbb_ref):
    @pl.when(pl.program_id(1) == 0)
    def _():
        wab_ref[...] = wa_ref[...].astype(BF16)
        wbb_ref[...] = wb_ref[...].astype(BF16)

    h = a_ref[...]
    a = _dot(h, wab_ref[...])
    b = _dot(h, wbb_ref[...])
    o_ref[...] = ((a * _sigmoid(a)) * b).astype(o_ref.dtype)


def _ffn_in(st, a, w, widx):
    m, k = a.shape
    hidden = w.shape[-1] // 2
    tm, tn = st.row_tile, FFN_COL_TILE
    nj = hidden // tn
    wspec = lambda off: pl.BlockSpec((None, None, k, tn), lambda j, i: widx + (0, j + off))
    return pl.pallas_call(
        _ffn_in_kernel,
        grid=(nj, m // tm),
        in_specs=[pl.BlockSpec((tm, k), lambda j, i: (i, 0)), wspec(0), wspec(nj)],
        out_specs=pl.BlockSpec((tm, tn), lambda j, i: (i, j)),
        out_shape=jax.ShapeDtypeStruct((m, hidden), BF16),
        scratch_shapes=[pltpu.VMEM((k, tn), BF16), pltpu.VMEM((k, tn), BF16)],
        compiler_params=_params("arbitrary", "arbitrary"),
        name="ffn_in",
    )(a, w, w)


def _mm_out_kernel(u_ref, w_ref, x_ref, gate_ref, o_ref, wb_ref, *, coef):
    @pl.when(pl.program_id(1) == 0)
    def _():
        wb_ref[...] = w_ref[...].astype(BF16)

    y = _dot(u_ref[...], wb_ref[...])
    gate = gate_ref[...]
    if coef != 1.0:
        gate = coef * gate
    o_ref[...] = x_ref[...] + gate * y


def _out_proj(st, u, w, widx, x, layer, sub, coef):
    m, k = u.shape
    d = x.shape[1]
    tm, tn = st.out_row_tile, OUT_COL_TILE
    nlead = len(widx)
    return pl.pallas_call(
        functools.partial(_mm_out_kernel, coef=coef),
        grid=(d // tn, m // tm),
        in_specs=[
            pl.BlockSpec((tm, k), lambda j, i: (i, 0)),
            pl.BlockSpec((None,) * nlead + (k, tn), lambda j, i: widx + (0, j)),
            pl.BlockSpec((tm, tn), lambda j, i: (i, j)),
            st.mod_spec(layer, 3 * sub + 2, tm, tn, lambda j, i: j, lambda j, i: i),
        ],
        out_specs=pl.BlockSpec((tm, tn), lambda j, i: (i, j)),
        out_shape=jax.ShapeDtypeStruct((m, d), F32),
        scratch_shapes=[pltpu.VMEM((k, tn), BF16)],
        compiler_params=_params("arbitrary", "arbitrary"),
        name="out_proj",
    )(u, w, x, st.mod)


def _sb_block(q, kb, vb, tri, bias, scale, carry, mask):
    z = _dot_nt(q, kb) * scale + bias
    sp = _softplus(z)
    log_not = -sp
    if mask is not None:
        log_not = jnp.where(mask, log_not, 0.0)
    hi, lo = _split_bf16(log_not)
    between = _dot(hi, tri) + _dot(lo, tri) + carry
    a = jnp.exp((z - sp) + between)
    if mask is not None:
        a = jnp.where(mask, a, 0.0)
    new_carry = between[:, 0:1] + log_not[:, 0:1]
    return _dot(a.astype(BF16), vb), new_carry


def _sb_prompt_kernel(bias_ref, q_ref, k_ref, v_ref, o_ref, kb_ref, vb_ref, *, tq, tk, scale):
    i = pl.program_id(2)

    @pl.when(i == 0)
    def _():
        kb_ref[...] = k_ref[...].astype(BF16)
        vb_ref[...] = v_ref[...].astype(BF16)

    bias = bias_ref[pl.program_id(1)]
    q = q_ref[...].astype(BF16)
    row = lax.broadcasted_iota(jnp.int32, (tk, tk), 0)
    col = lax.broadcasted_iota(jnp.int32, (tk, tk), 1)
    tri = jnp.where(row > col, 1.0, 0.0).astype(BF16)
    causal = col < row

    def key_block(j):
        start = pl.multiple_of(j * tk, tk)
        return kb_ref[pl.ds(start, tk), :], vb_ref[pl.ds(start, tk), :]

    kb, vb = key_block(i)
    acc, carry = _sb_block(q, kb, vb, tri, bias, scale, jnp.zeros((tq, 1), F32), causal)

    def body(t, state):
        acc, carry = state
        kb, vb = key_block(i - 1 - t)
        out, carry = _sb_block(q, kb, vb, tri, bias, scale, carry, None)
        return acc + out, carry

    acc, _ = lax.fori_loop(0, i, body, (acc, carry))
    o_ref[...] = acc.astype(o_ref.dtype)


def _sb_prompt(qkv, bias, batch, seq, layer_j):
    _, m, width = qkv.shape
    dh = width // SB_HEADS
    tq = tk = SB_Q_TILE
    assert SB_Q_TILE == SB_K_TILE
    qkv4 = qkv.reshape(3, batch, seq, width)
    kv_spec = lambda part: pl.BlockSpec((None, None, seq, dh), lambda b, h, i: (part, b, 0, h))
    out = pl.pallas_call(
        functools.partial(_sb_prompt_kernel, tq=tq, tk=tk, scale=dh ** -0.5),
        grid=(batch, SB_HEADS, seq // tq),
        in_specs=[
            pl.BlockSpec(memory_space=pltpu.SMEM),
            pl.BlockSpec((None, None, tq, dh), lambda b, h, i: (0, b, i, h)),
            kv_spec(1),
            kv_spec(2),
        ],
        out_specs=pl.BlockSpec((None, tq, dh), lambda b, h, i: (b, i, h)),
        out_shape=jax.ShapeDtypeStruct((batch, seq, width), BF16),
        scratch_shapes=[pltpu.VMEM((seq, dh), BF16), pltpu.VMEM((seq, dh), BF16)],
        compiler_params=_params("arbitrary", "arbitrary", "arbitrary"),
        name="sb_prompt",
    )(bias, qkv4, qkv4, qkv4)
    return out.reshape(m, width)


def _sb_sample_kernel(pt_ref, qbd_ref, bias_ref, knew_ref, vnew_ref, *refs, n_pages_step, n_new, scale):
    k_refs = refs[:n_pages_step]
    v_refs = refs[n_pages_step:2 * n_pages_step]
    o_ref, acc_ref, carry_ref = refs[2 * n_pages_step:]
    s = pl.program_id(1)
    qbd = qbd_ref[...]
    bias = bias_ref[...]
    nl = qbd.shape[1]

    @pl.when(s == 0)
    def _():
        z = _dot(knew_ref[...].astype(BF16), qbd) * scale + bias
        sp = _softplus(z)
        key = lax.broadcasted_iota(jnp.int32, (n_new, nl), 0)
        qry = lax.broadcasted_iota(jnp.int32, (n_new, nl), 1) % n_new
        mask = key < qry
        log_not = jnp.where(mask, -sp, 0.0)
        r = lax.broadcasted_iota(jnp.int32, (n_new, n_new), 0)
        c = lax.broadcasted_iota(jnp.int32, (n_new, n_new), 1)
        upper = jnp.where(c > r, 1.0, 0.0).astype(BF16)
        hi, lo = _split_bf16(log_not)
        between = _dot(upper, hi) + _dot(upper, lo)
        a = jnp.where(mask, jnp.exp((z - sp) + between), 0.0)
        acc_ref[...] = _dot_tn(a.astype(BF16), vnew_ref[...].astype(BF16))
        carry_ref[...] = between[0:1, :] + log_not[0:1, :]

    tp = k_refs[0].shape[0]
    r = lax.broadcasted_iota(jnp.int32, (tp, tp), 0)
    c = lax.broadcasted_iota(jnp.int32, (tp, tp), 1)
    upper = jnp.where(c > r, 1.0, 0.0).astype(BF16)
    carry = carry_ref[...]
    acc = acc_ref[...]
    for g in range(n_pages_step):
        z = _dot(k_refs[g][...].astype(BF16), qbd) * scale + bias
        sp = _softplus(z)
        log_not = -sp
        hi, lo = _split_bf16(log_not)
        between = _dot(upper, hi) + _dot(upper, lo) + carry
        a = jnp.exp((z - sp) + between)
        acc = acc + _dot_tn(a.astype(BF16), v_refs[g][...].astype(BF16))
        carry = between[0:1, :] + log_not[0:1, :]
    acc_ref[...] = acc
    carry_ref[...] = carry

    @pl.when(s == pl.num_programs(1) - 1)
    def _():
        dh = qbd.shape[0] // SB_HEADS
        for h in range(SB_HEADS):
            o_ref[:, h * dh:(h + 1) * dh] = acc_ref[h * n_new:(h + 1) * n_new, h * dh:(h + 1) * dh].astype(o_ref.dtype)


def _sb_sample(qkv, bias, cache_k, cache_v, page_table, batch, n_new, layer_j):
    _, m, width = qkv.shape
    dh = width // SB_HEADS
    n_layers, n_pool, page = cache_k.shape[:3]
    n_pages = page_table.shape[1]
    g = PAGES_PER_STEP
    assert n_pages % g == 0
    q = qkv[0].reshape(batch, n_new, SB_HEADS, dh)
    eye = jnp.eye(SB_HEADS, dtype=F32)
    qbd = jnp.einsum("bqhe,hg->bhegq", q, eye).reshape(batch, width, SB_HEADS * n_new).astype(BF16)
    bias_l = jnp.repeat(bias, n_new).reshape(1, SB_HEADS * n_new)
    qkv4 = qkv.reshape(3, batch, n_new, width)
    ck = cache_k.reshape(n_layers, n_pool, page, width)
    cv = cache_v.reshape(n_layers, n_pool, page, width)

    def page_spec(gi):
        return pl.BlockSpec((None, None, page, width),
                            lambda b, s, pt: (layer_j, pt[b, n_pages - 1 - (s * g + gi)], 0, 0))

    grid_spec = pltpu.PrefetchScalarGridSpec(
        num_scalar_prefetch=1,
        grid=(batch, n_pages // g),
        in_specs=[
            pl.BlockSpec((None, width, SB_HEADS * n_new), lambda b, s, pt: (b, 0, 0)),
            pl.BlockSpec((1, SB_HEADS * n_new), lambda b, s, pt: (0, 0)),
            pl.BlockSpec((None, None, n_new, width), lambda b, s, pt: (1, b, 0, 0)),
            pl.BlockSpec((None, None, n_new, width), lambda b, s, pt: (2, b, 0, 0)),
        ] + [page_spec(gi) for gi in range(g)] * 2,
        out_specs=pl.BlockSpec((None, n_new, width), lambda b, s, pt: (b, 0, 0)),
        scratch_shapes=[pltpu.VMEM((SB_HEADS * n_new, width), F32), pltpu.VMEM((1, SB_HEADS * n_new), F32)],
    )
    out = pl.pallas_call(
        functools.partial(_sb_sample_kernel, n_pages_step=g, n_new=n_new, scale=dh ** -0.5),
        grid_spec=grid_spec,
        out_shape=jax.ShapeDtypeStruct((batch, n_new, width), BF16),
        compiler_params=_params("arbitrary", "arbitrary"),
        name="sb_sample",
    )(page_table, qbd, bias_l, qkv4, qkv4, *([ck] * g), *([cv] * g))
    return out.reshape(m, width)


def _gla_gate_kernel(h_ref, wr_ref, wg_ref, b_ref, o_ref):
    glr = _dot(h_ref[...], wr_ref[...].astype(BF16))
    x = _dot(glr.astype(BF16), wg_ref[...].astype(BF16)) + b_ref[...]
    o_ref[...] = -_softplus(-x) / GLA_TAU


def _gla_gate(st, h, w_in, w_gate, b_gate, j):
    m, d = h.shape
    dk = w_gate.shape[-1]
    rank = w_gate.shape[1]
    w_r = jnp.pad(w_in[j][:, w_in.shape[-1] - rank:], ((0, 0), (0, LANES - rank)))
    w_g = jnp.pad(w_gate[j], ((0, LANES - rank), (0, 0)))
    tm = st.norm_row_tile
    return pl.pallas_call(
        _gla_gate_kernel,
        grid=(m // tm,),
        in_specs=[
            pl.BlockSpec((tm, d), lambda i: (i, 0)),
            pl.BlockSpec((d, LANES), lambda i: (0, 0)),
            pl.BlockSpec((LANES, dk), lambda i: (0, 0)),
            pl.BlockSpec((1, dk), lambda i: (0, 0)),
        ],
        out_specs=pl.BlockSpec((tm, dk), lambda i: (i, 0)),
        out_shape=jax.ShapeDtypeStruct((m, dk), F32),
        compiler_params=_params("arbitrary"),
        name="gla_gate",
    )(h, w_r, w_g, b_gate[j].reshape(1, dk))


def _gla_kernel(q_ref, k_ref, v_ref, r_ref, g_ref, ng_ref, s0_ref, o_ref, s_ref, *, chunk, n_chunks, scale):
    s_ref[...] = s0_ref[...]
    row = lax.broadcasted_iota(jnp.int32, (chunk, chunk), 0)
    col = lax.broadcasted_iota(jnp.int32, (chunk, chunk), 1)
    keep = col <= row
    lower = jnp.where(keep, 1.0, 0.0).astype(BF16)
    ones = jnp.ones((chunk, LANES), BF16)
    ng = ng_ref[...]
    dv = v_ref.shape[-1]

    def body(c, _):
        rows = slice(None) if n_chunks == 1 else pl.ds(pl.multiple_of(c * chunk, chunk), chunk)
        g = g_ref[rows, :]
        g_hi, g_lo = _split_bf16(g)
        bcum = _dot(lower, g_hi) + _dot(lower, g_lo)
        b_last = bcum[chunk - 1:chunk, :]
        q = q_ref[rows, :] * scale
        k = k_ref[rows, :]
        v = v_ref[rows, :].astype(BF16)
        qe = (q * jnp.exp(bcum)).astype(BF16)
        ke = (k * jnp.exp(-bcum)).astype(BF16)
        kl = (k * jnp.exp(b_last - bcum)).astype(BF16)
        state = s_ref[...]
        scores = jnp.where(keep, _dot_nt(qe, ke), 0.0)
        o = _dot(qe, state.astype(BF16)) + _dot(scores.astype(BF16), v)
        total = _dot_tn(g_hi, ones) + _dot_tn(g_lo, ones)
        decay = jnp.exp(total)
        s_ref[...] = jnp.concatenate([decay] * (dv // LANES), axis=1) * state + _dot_tn(kl, v)
        on = o * lax.rsqrt(jnp.mean(o * o, axis=-1, keepdims=True) + RMS_EPS) * ng
        r = r_ref[rows, :]
        o_ref[rows, :] = (on * (r * _sigmoid(r))).astype(o_ref.dtype)
        return 0

    if n_chunks == 1:
        body(0, 0)
    else:
        lax.fori_loop(0, n_chunks, body, 0)


def _gla(proj, g, norm_g, s0, batch, seq):
    _, m, d = proj.shape
    dk = g.shape[1]
    dkh, dvh = dk // GLA_HEADS, d // GLA_HEADS
    chunk = min(GLA_CHUNK, seq)
    p4 = proj.reshape(3, batch, seq, d)
    g3 = g.reshape(batch, seq, dk)
    nkh = dk // dkh
    out, state = pl.pallas_call(
        functools.partial(_gla_kernel, chunk=chunk, n_chunks=seq // chunk, scale=dkh ** -0.5),
        grid=(batch, GLA_HEADS),
        in_specs=[
            pl.BlockSpec((None, None, seq, dkh), lambda b, h: (0, b, 0, h)),
            pl.BlockSpec((None, None, seq, dkh), lambda b, h: (0, b, 0, nkh + h)),
            pl.BlockSpec((None, None, seq, dvh), lambda b, h: (1, b, 0, h)),
            pl.BlockSpec((None, None, seq, dvh), lambda b, h: (2, b, 0, h)),
            pl.BlockSpec((None, seq, dkh), lambda b, h: (b, 0, h)),
            pl.BlockSpec((1, dvh), lambda b, h: (0, 0)),
            pl.BlockSpec((None, None, dkh, dvh), lambda b, h: (b, h, 0, 0)),
        ],
        out_specs=[
            pl.BlockSpec((None, seq, dvh), lambda b, h: (b, 0, h)),
            pl.BlockSpec((None, None, dkh, dvh), lambda b, h: (b, h, 0, 0)),
        ],
        out_shape=[
            jax.ShapeDtypeStruct((batch, seq, d), BF16),
            jax.ShapeDtypeStruct((batch, GLA_HEADS, dkh, dvh), F32),
        ],
        compiler_params=_params("arbitrary", "arbitrary"),
        name="gla",
    )(p4, p4, p4, p4, g3, norm_g.reshape(1, dvh), s0)
    return out.reshape(m, d), state


def _conv_kernel(gb_ref, gc_ref, u_ref, w_ref, buf_ref, o_ref, tail_ref, z_ref, *, tt, n_prev):
    i = pl.program_id(1)

    @pl.when(i == 0)
    def _():
        z_ref[SUBLANES - n_prev:SUBLANES, :] = buf_ref[...]

    z = gc_ref[...] * u_ref[...]
    z_ref[SUBLANES:SUBLANES + tt, :] = z
    conv = w_ref[n_prev:n_prev + 1, :] * z
    for j in range(n_prev):
        conv = conv + w_ref[j:j + 1, :] * z_ref[SUBLANES - n_prev + j:SUBLANES - n_prev + j + tt, :]
    o_ref[...] = (gb_ref[...] * conv).astype(o_ref.dtype)
    last = z_ref[tt:tt + SUBLANES, :]
    z_ref[0:SUBLANES, :] = last
    tail_ref[...] = last[SUBLANES - n_prev:, :]


def _conv(proj, w_conv, buf, batch, seq):
    _, m, d = proj.shape
    n_prev = CONV_WIDTH - 1
    tt = min(seq, NORM_ROW_TILE)
    p4 = proj.reshape(3, batch, seq, d)
    part = lambda p: pl.BlockSpec((None, None, tt, d), lambda b, i: (p, b, i, 0))
    out, tail = pl.pallas_call(
        functools.partial(_conv_kernel, tt=tt, n_prev=n_prev),
        grid=(batch, seq // tt),
        in_specs=[
            part(0), part(1), part(2),
            pl.BlockSpec((CONV_WIDTH, d), lambda b, i: (0, 0)),
            pl.BlockSpec((None, n_prev, d), lambda b, i: (b, 0, 0)),
        ],
        out_specs=[
            pl.BlockSpec((None, tt, d), lambda b, i: (b, i, 0)),
            pl.BlockSpec((None, n_prev, d), lambda b, i: (b, 0, 0)),
        ],
        out_shape=[
            jax.ShapeDtypeStruct((batch, seq, d), BF16),
            jax.ShapeDtypeStruct((batch, n_prev, d), F32),
        ],
        scratch_shapes=[pltpu.VMEM((tt + SUBLANES, d), F32)],
        compiler_params=_params("arbitrary", "arbitrary"),
        name="short_conv",
    )(p4, p4, p4, w_conv, buf)
    return out.reshape(m, d), tail


def kernel(x_prompt, x_sample, c_prompt, c_sample, cache_sb_k, cache_sb_v, page_table, state_gla, state_conv,
           norm_g, w_mod, b_mod, ffn_w_in, ffn_w_out, sb_w_in, sb_w_out, sb_bias, gla_w_in, gla_w_gate, gla_b_gate,
           gla_norm_g, gla_w_out, conv_w_in, conv_w, conv_w_out, final_norm_g):
    batch, seq, d = x_prompt.shape
    dec_b, dec_seq, _ = x_sample.shape
    depth = norm_g.shape[0]
    dh = d // SB_HEADS

    n_seq = batch + dec_b
    pad = (-n_seq) % SUBLANES
    c_all = jnp.concatenate([c_prompt, c_sample, jnp.zeros((pad, d), F32)], axis=0)
    mods = _modulation(c_all, w_mod, b_mod)
    mod_p = mods[:, :batch].reshape(depth, batch, N_MOD, 1, d)
    mod_s = jnp.repeat(mods[:, batch:n_seq], dec_seq, axis=1).reshape(depth, 1, dec_b * dec_seq, N_MOD, d)
    mod_s = mod_s.transpose(0, 1, 3, 2, 4)

    rows_s = dec_b * dec_seq
    st_p = _Stream(batch, seq, mod_p, ROW_TILE, OUT_ROW_TILE, NORM_ROW_TILE)
    st_s = _Stream(dec_b, dec_seq, mod_s, rows_s, rows_s, rows_s)
    xp = x_prompt.reshape(batch * seq, d)
    xs = x_sample.reshape(rows_s, d)

    def ffn_half(st, x, l, sub, which):
        h = _norm_mod(st, x, norm_g[l, sub], l, sub)
        u = _ffn_in(st, h, ffn_w_in, (l, which))
        return _out_proj(st, u, ffn_w_out, (l, which), x, l, sub, 0.5)

    sbk_p, sbv_p, sbk_s, sbv_s = [], [], [], []
    gla_p, gla_s, conv_p, conv_s = [], [], [], []

    for l in range(depth):
        xp = ffn_half(st_p, xp, l, 0, 0)
        xs = ffn_half(st_s, xs, l, 0, 0)
        hp = _norm_mod(st_p, xp, norm_g[l, 1], l, 1)
        hs = _norm_mod(st_s, xs, norm_g[l, 1], l, 1)
        kind, j = l % N_MIXERS, l // N_MIXERS
        if kind == 0:
            qkv_p = _in_proj(st_p, hp, sb_w_in, (j,), 3)
            qkv_s = _in_proj(st_s, hs, sb_w_in, (j,), 3)
            op = _sb_prompt(qkv_p, sb_bias[j], batch, seq, j)
            os_ = _sb_sample(qkv_s, sb_bias[j], cache_sb_k, cache_sb_v, page_table, dec_b, dec_seq, j)
            sbk_p.append(qkv_p[1].reshape(batch, seq, SB_HEADS, dh))
            sbv_p.append(qkv_p[2].reshape(batch, seq, SB_HEADS, dh))
            sbk_s.append(qkv_s[1].reshape(dec_b, dec_seq, SB_HEADS, dh))
            sbv_s.append(qkv_s[2].reshape(dec_b, dec_seq, SB_HEADS, dh))
            w_out, widx = sb_w_out, (j,)
        elif kind == 1:
            pr_p = _in_proj(st_p, hp, gla_w_in, (j,), 3)
            pr_s = _in_proj(st_s, hs, gla_w_in, (j,), 3)
            g_p = _gla_gate(st_p, hp, gla_w_in, gla_w_gate, gla_b_gate, j)
            g_s = _gla_gate(st_s, hs, gla_w_in, gla_w_gate, gla_b_gate, j)
            s0 = jnp.zeros((batch,) + state_gla.shape[2:], F32)
            op, sp = _gla(pr_p, g_p, gla_norm_g[j], s0, batch, seq)
            os_, ss = _gla(pr_s, g_s, gla_norm_g[j], state_gla[j], dec_b, dec_seq)
            gla_p.append(sp)
            gla_s.append(ss)
            w_out, widx = gla_w_out, (j,)
        else:
            pr_p = _in_proj(st_p, hp, conv_w_in, (j,), 3)
            pr_s = _in_proj(st_s, hs, conv_w_in, (j,), 3)
            b0 = jnp.zeros((batch, CONV_WIDTH - 1, d), F32)
            op, bp = _conv(pr_p, conv_w[j], b0, batch, seq)
            os_, bs = _conv(pr_s, conv_w[j], state_conv[j], dec_b, dec_seq)
            conv_p.append(bp)
            conv_s.append(bs)
            w_out, widx = conv_w_out, (j,)
        xp = _out_proj(st_p, op, w_out, widx, xp, l, 1, 1.0)
        xs = _out_proj(st_s, os_, w_out, widx, xs, l, 1, 1.0)
        xp = ffn_half(st_p, xp, l, 2, 1)
        xs = ffn_half(st_s, xs, l, 2, 1)

    y_prompt = _final_norm(st_p, xp, final_norm_g).reshape(batch, seq, d)
    y_sample = _final_norm(st_s, xs, final_norm_g).reshape(dec_b, dec_seq, d)
    return (y_prompt, y_sample, jnp.stack(sbk_p), jnp.stack(sbv_p), jnp.stack(sbk_s), jnp.stack(sbv_s),
            jnp.stack(gla_p), jnp.stack(gla_s), jnp.stack(conv_p), jnp.stack(conv_s))
```

```python
import functools

import jax
import jax.numpy as jnp
from jax import lax
from jax.experimental import pallas as pl
from jax.experimental.pallas import tpu as pltpu

F32 = jnp.float32
BF16 = jnp.bfloat16

N_MIXERS = 3
N_MOD = 9
SB_HEADS = 16
GLA_HEADS = 4
GLA_RANK = 16
GLA_TAU = 16.0
GLA_CHUNK = 64
CONV_WIDTH = 3
RMS_EPS = 1e-6
LOG2E = 1.4426950408889634

LANES = 128
SUBLANES = 8
VMEM_BYTES_V7X = 64 * 1024 * 1024
VMEM_LIMIT = VMEM_BYTES_V7X - 8 * 1024 * 1024

ROW_TILE = 1024
COL_TILE = 1024
QKV_COL_TILE = 512
FFN_ROW_TILE = 2048
FFN_COL_TILE = 512
OUT_TILE_DEEP = 512
OUT_TILE = 1024
NORM_ROW_TILE = 1024
CONV_ROW_TILE = 512
MOD_COL_TILE = 1024
SB_K_TILE = 256
PAGES_PER_STEP = 8


def _params(*sem):
    return pltpu.CompilerParams(dimension_semantics=sem, vmem_limit_bytes=VMEM_LIMIT)


def _sigmoid(x):
    return 1.0 / (1.0 + jnp.exp(-x))


def _softplus(z):
    return jnp.maximum(z, 0.0) + jnp.log1p(jnp.exp(-jnp.abs(z)))


def _split_bf16(x):
    hi = x.astype(BF16)
    lo = (x - hi.astype(F32)).astype(BF16)
    return hi, lo


def _dot(a, b):
    return jnp.dot(a, b, preferred_element_type=F32)


def _dot_nt(a, b):
    return lax.dot_general(a, b, (((1,), (1,)), ((), ())), preferred_element_type=F32)


def _dot_tn(a, b):
    return lax.dot_general(a, b, (((0,), (0,)), ((), ())), preferred_element_type=F32)


def _mod_kernel(c_ref, w_ref, b_ref, o_ref):
    c = c_ref[...]
    s = (c * _sigmoid(c)).astype(BF16)
    o_ref[...] = _dot(s, w_ref[...].astype(BF16)) + b_ref[...]


def _modulation(c_all, w_mod, b_mod):
    depth, d, n = w_mod.shape
    rows = c_all.shape[0]
    tn = MOD_COL_TILE
    return pl.pallas_call(
        _mod_kernel,
        grid=(depth, n // tn),
        in_specs=[
            pl.BlockSpec((rows, d), lambda l, j: (0, 0)),
            pl.BlockSpec((None, d, tn), lambda l, j: (l, 0, j)),
            pl.BlockSpec((None, 1, tn), lambda l, j: (l, 0, j)),
        ],
        out_specs=pl.BlockSpec((None, rows, tn), lambda l, j: (l, 0, j)),
        out_shape=jax.ShapeDtypeStruct((depth, rows, n), F32),
        compiler_params=_params("arbitrary", "arbitrary"),
        name="modulation",
    )(c_all, w_mod, b_mod.reshape(depth, 1, n))


class _Stream:
    def __init__(self, batch, seq, mod, row_tile, norm_row_tile):
        self.batch, self.seq, self.mod = batch, seq, mod
        self.rows = batch * seq
        self.row_tile, self.norm_row_tile = row_tile, norm_row_tile
        self.per_tile_mod = mod.shape[3] != 1

    def mod_spec(self, layer, chunk, tm, width, col_of, row_of):
        r = self.mod.shape[3]
        if self.per_tile_mod:
            assert r == tm
            group = lambda *g: 0
        else:
            tiles_per_seq = self.seq // tm
            group = lambda *g: row_of(*g) // tiles_per_seq
        return pl.BlockSpec((None, None, None, r, width),
                            lambda *g: (layer, group(*g), chunk, 0, col_of(*g)))


def _norm_mod_kernel(x_ref, g_ref, shift_ref, scale_ref, o_ref):
    x = x_ref[...]
    y = x * lax.rsqrt(jnp.mean(x * x, axis=-1, keepdims=True) + RMS_EPS) * g_ref[...]
    o_ref[...] = (y * (1.0 + scale_ref[...]) + shift_ref[...]).astype(o_ref.dtype)


def _norm_mod(st, x, g, layer, sub):
    m, d = x.shape
    tm = st.norm_row_tile
    row_of = lambda i: i
    col_of = lambda i: 0
    return pl.pallas_call(
        _norm_mod_kernel,
        grid=(m // tm,),
        in_specs=[
            pl.BlockSpec((tm, d), lambda i: (i, 0)),
            pl.BlockSpec((1, d), lambda i: (0, 0)),
            st.mod_spec(layer, 3 * sub, tm, d, col_of, row_of),
            st.mod_spec(layer, 3 * sub + 1, tm, d, col_of, row_of),
        ],
        out_specs=pl.BlockSpec((tm, d), lambda i: (i, 0)),
        out_shape=jax.ShapeDtypeStruct((m, d), BF16),
        compiler_params=_params("arbitrary"),
        name="norm_mod",
    )(x, g.reshape(1, d), st.mod, st.mod)


def _final_norm_kernel(x_ref, g_ref, o_ref):
    x = x_ref[...]
    o_ref[...] = x * lax.rsqrt(jnp.mean(x * x, axis=-1, keepdims=True) + RMS_EPS) * g_ref[...]


def _final_norm(st, x, g):
    m, d = x.shape
    tm = st.norm_row_tile
    return pl.pallas_call(
        _final_norm_kernel,
        grid=(m // tm,),
        in_specs=[pl.BlockSpec((tm, d), lambda i: (i, 0)), pl.BlockSpec((1, d), lambda i: (0, 0))],
        out_specs=pl.BlockSpec((tm, d), lambda i: (i, 0)),
        out_shape=jax.ShapeDtypeStruct((m, d), F32),
        compiler_params=_params("arbitrary"),
        name="final_norm",
    )(x, g.reshape(1, d))


def _mm_kernel(a_ref, as_ref, w_ref, o_ref, os_ref, wb_ref):
    @pl.when(pl.program_id(1) == 0)
    def _():
        wb_ref[...] = w_ref[...].astype(BF16)
        os_ref[...] = _dot(as_ref[...], wb_ref[...])

    o_ref[...] = _dot(a_ref[...], wb_ref[...])


def _in_proj(st, a, a_s, w, widx, n_parts):
    m, k = a.shape
    ms = a_s.shape[0]
    tm, tn = st.row_tile, COL_TILE
    tpp = k // tn
    nlead = len(widx)
    return pl.pallas_call(
        _mm_kernel,
        grid=(n_parts * tpp, m // tm),
        in_specs=[
            pl.BlockSpec((tm, k), lambda j, i: (i, 0)),
            pl.BlockSpec((ms, k), lambda j, i: (0, 0)),
            pl.BlockSpec((None,) * nlead + (k, tn), lambda j, i: widx + (0, j)),
        ],
        out_specs=[
            pl.BlockSpec((None, tm, tn), lambda j, i: (j // tpp, i, j % tpp)),
            pl.BlockSpec((None, ms, tn), lambda j, i: (j // tpp, 0, j % tpp)),
        ],
        out_shape=[jax.ShapeDtypeStruct((n_parts, m, k), F32), jax.ShapeDtypeStruct((n_parts, ms, k), F32)],
        scratch_shapes=[pltpu.VMEM((k, tn), BF16)],
        compiler_params=_params("arbitrary", "arbitrary"),
        name="in_proj",
    )(a, a_s, w)


def _qkv_kernel(a_ref, as_ref, w_ref, *refs, tpp, aliased):
    q_ref, k_ref, v_ref, os_ref, wb_ref = refs[2 if aliased else 0:]
    part = pl.program_id(0) // tpp

    @pl.when(pl.program_id(1) == 0)
    def _():
        wb_ref[...] = w_ref[...].astype(BF16)
        os_ref[...] = _dot(as_ref[...], wb_ref[...])

    y = _dot(a_ref[...], wb_ref[...])
    for p, ref in enumerate((q_ref, k_ref, v_ref)):
        @pl.when(part == p)
        def _(ref=ref):
            ref[...] = y


def _qkv_proj(st, a, a_s, w, layer_j, n_layers, k_all, v_all):
    m, k = a.shape
    ms = a_s.shape[0]
    tm, tn = st.row_tile, QKV_COL_TILE
    tpp = k // tn
    ni = m // tm
    aliased = k_all is not None

    def rows(j, i, part):
        return jnp.where(j < part * tpp, 0, jnp.where(j < (part + 1) * tpp, i, ni - 1))

    def cols(j, part):
        return jnp.clip(j - part * tpp, 0, tpp - 1)

    stacked = jax.ShapeDtypeStruct((n_layers, m, k), F32)
    any_spec = pl.BlockSpec(memory_space=pl.ANY)
    q, k_all, v_all, out_s = pl.pallas_call(
        functools.partial(_qkv_kernel, tpp=tpp, aliased=aliased),
        grid=(3 * tpp, ni),
        in_specs=[
            pl.BlockSpec((tm, k), lambda j, i: (i, 0)),
            pl.BlockSpec((ms, k), lambda j, i: (0, 0)),
            pl.BlockSpec((None, k, tn), lambda j, i: (layer_j, 0, j)),
        ] + ([any_spec, any_spec] if aliased else []),
        out_specs=[
            pl.BlockSpec((tm, tn), lambda j, i: (rows(j, i, 0), cols(j, 0))),
            pl.BlockSpec((None, tm, tn), lambda j, i: (layer_j, rows(j, i, 1), cols(j, 1))),
            pl.BlockSpec((None, tm, tn), lambda j, i: (layer_j, rows(j, i, 2), cols(j, 2))),
            pl.BlockSpec((None, ms, tn), lambda j, i: (j // tpp, 0, j % tpp)),
        ],
        out_shape=[jax.ShapeDtypeStruct((m, k), F32), stacked, stacked, jax.ShapeDtypeStruct((3, ms, k), F32)],
        scratch_shapes=[pltpu.VMEM((k, tn), BF16)],
        input_output_aliases={3: 1, 4: 2} if aliased else {},
        compiler_params=_params("arbitrary", "arbitrary"),
        name="qkv_proj",
    )(a, a_s, w, *((k_all, v_all) if aliased else ()))
    return q, k_all, v_all, out_s


def _swiglu(h, wa, wb):
    a = _dot(h, wa)
    return (a * _sigmoid(a)) * _dot(h, wb)


def _ffn_in_kernel(a_ref, as_ref, wa_ref, wb_ref, o_ref, os_ref, wab_ref, wbb_ref):
    @pl.when(pl.program_id(1) == 0)
    def _():
        wab_ref[...] = wa_ref[...].astype(BF16)
        wbb_ref[...] = wb_ref[...].astype(BF16)
        os_ref[...] = _swiglu(as_ref[...], wab_ref[...], wbb_ref[...]).astype(os_ref.dtype)

    o_ref[...] = _swiglu(a_ref[...], wab_ref[...], wbb_ref[...]).astype(o_ref.dtype)


def _ffn_in(st, a, a_s, w, widx):
    m, k = a.shape
    ms = a_s.shape[0]
    hidden = w.shape[-1] // 2
    tm, tn = min(FFN_ROW_TILE, m), FFN_COL_TILE
    nj = hidden // tn
    wspec = lambda off: pl.BlockSpec((None, None, k, tn), lambda j, i: widx + (0, j + off))
    return pl.pallas_call(
        _ffn_in_kernel,
        grid=(nj, m // tm),
        in_specs=[pl.BlockSpec((tm, k), lambda j, i: (i, 0)), pl.BlockSpec((ms, k), lambda j, i: (0, 0)),
                  wspec(0), wspec(nj)],
        out_specs=[pl.BlockSpec((tm, tn), lambda j, i: (i, j)), pl.BlockSpec((ms, tn), lambda j, i: (0, j))],
        out_shape=[jax.ShapeDtypeStruct((m, hidden), BF16), jax.ShapeDtypeStruct((ms, hidden), BF16)],
        scratch_shapes=[pltpu.VMEM((k, tn), BF16), pltpu.VMEM((k, tn), BF16)],
        compiler_params=_params("arbitrary", "arbitrary"),
        name="ffn_in",
    )(a, a_s, w, w)


def _mm_out_kernel(u_ref, us_ref, w_ref, x_ref, xs_ref, gate_ref, gates_ref, o_ref, os_ref, wb_ref, *, coef):
    def residual(u_ref, x_ref, gate_ref):
        gate = gate_ref[...]
        if coef != 1.0:
            gate = coef * gate
        return x_ref[...] + gate * _dot(u_ref[...], wb_ref[...])

    @pl.when(pl.program_id(1) == 0)
    def _():
        wb_ref[...] = w_ref[...].astype(BF16)
        os_ref[...] = residual(us_ref, xs_ref, gates_ref)

    o_ref[...] = residual(u_ref, x_ref, gate_ref)


def _out_proj(st, st_s, u, u_s, w, widx, x, x_s, layer, sub, coef):
    m, k = u.shape
    ms = u_s.shape[0]
    d = x.shape[1]
    tm = tn = OUT_TILE if k <= d else OUT_TILE_DEEP
    nlead = len(widx)
    col_of = lambda j, i: j
    return pl.pallas_call(
        functools.partial(_mm_out_kernel, coef=coef),
        grid=(d // tn, m // tm),
        in_specs=[
            pl.BlockSpec((tm, k), lambda j, i: (i, 0)),
            pl.BlockSpec((ms, k), lambda j, i: (0, 0)),
            pl.BlockSpec((None,) * nlead + (k, tn), lambda j, i: widx + (0, j)),
            pl.BlockSpec((tm, tn), lambda j, i: (i, j)),
            pl.BlockSpec((ms, tn), lambda j, i: (0, j)),
            st.mod_spec(layer, 3 * sub + 2, tm, tn, col_of, lambda j, i: i),
            st_s.mod_spec(layer, 3 * sub + 2, ms, tn, col_of, lambda j, i: 0),
        ],
        out_specs=[pl.BlockSpec((tm, tn), lambda j, i: (i, j)), pl.BlockSpec((ms, tn), lambda j, i: (0, j))],
        out_shape=[jax.ShapeDtypeStruct((m, d), F32), jax.ShapeDtypeStruct((ms, d), F32)],
        scratch_shapes=[pltpu.VMEM((k, tn), BF16)],
        compiler_params=_params("arbitrary", "arbitrary"),
        name="out_proj",
    )(u, u_s, w, x, x_s, st.mod, st_s.mod)


def _softplus2(z2):
    return jnp.maximum(z2, 0.0) + jnp.log(1.0 + jnp.exp2(-jnp.abs(z2))) * LOG2E


def _sb_block(q, kb, vb, carry, mask, *, tri, bias2, scale2):
    z = _dot_nt(q, kb) * scale2 + bias2
    sp = _softplus2(z)
    log_not = -sp
    if mask is not None:
        log_not = jnp.where(mask, log_not, 0.0)
    between = _dot(log_not.astype(BF16), tri) + carry
    a = jnp.exp2((z - sp) + between)
    if mask is not None:
        a = jnp.where(mask, a, 0.0)
    new_carry = between[:, 0:1] + log_not[:, 0:1]
    return _dot(a.astype(BF16), vb), new_carry


def _sb_prompt_kernel(bias_ref, q_ref, k_ref, v_ref, o_ref, kb_ref, vb_ref, qb_ref, acc_ref, carry_ref,
                      *, tk, scale2):
    kb_ref[...] = k_ref[...].astype(BF16)
    vb_ref[...] = v_ref[...].astype(BF16)
    qb_ref[...] = q_ref[...].astype(BF16)
    tq = 2 * tk
    row = lax.broadcasted_iota(jnp.int32, (tk, tk), 0)
    col = lax.broadcasted_iota(jnp.int32, (tk, tk), 1)
    tri = jnp.where(row > col, 1.0, 0.0).astype(BF16)
    causal = col < row
    block = functools.partial(_sb_block, tri=tri, bias2=bias_ref[pl.program_id(1)] * LOG2E, scale2=scale2)

    def keys(j):
        start = pl.multiple_of(j * tk, tk)
        return kb_ref[pl.ds(start, tk), :], vb_ref[pl.ds(start, tk), :]

    zero = jnp.zeros((tk, 1), F32)

    def query_tile(i, _):
        q_lo = qb_ref[pl.ds(pl.multiple_of(i * tq, tq), tk), :]
        q_hi = qb_ref[pl.ds(pl.multiple_of(i * tq + tk, tk), tk), :]
        kb, vb = keys(2 * i + 1)
        out_b, carry_b = block(q_hi, kb, vb, zero, causal)
        kb, vb = keys(2 * i)
        out_a, carry_a = block(q_lo, kb, vb, zero, causal)
        out_c, carry_c = block(q_hi, kb, vb, carry_b, None)
        acc_ref[:tk, :] = out_a
        acc_ref[tk:, :] = out_b + out_c
        carry_ref[:tk, :] = carry_a
        carry_ref[tk:, :] = carry_c

        def body(t, _):
            p = i - 1 - t
            q = qb_ref[pl.ds(pl.multiple_of(i * tq, tq), tq), :]
            kb, vb = keys(2 * p + 1)
            out1, carry = block(q, kb, vb, carry_ref[...], None)
            kb, vb = keys(2 * p)
            out2, carry = block(q, kb, vb, carry, None)
            acc_ref[...] += out1 + out2
            carry_ref[...] = carry
            return 0

        lax.fori_loop(0, i, body, 0)
        o_ref[pl.ds(pl.multiple_of(i * tq, tq), tq), :] = acc_ref[...].astype(o_ref.dtype)
        return 0

    lax.fori_loop(0, q_ref.shape[0] // tq, query_tile, 0)


def _sb_prompt(q, k_all, v_all, bias, batch, seq, layer_j):
    m, width = q.shape
    dh = width // SB_HEADS
    tk = SB_K_TILE
    tq = 2 * tk
    assert seq % tq == 0
    kv_spec = pl.BlockSpec((None, None, seq, dh), lambda b, h: (layer_j, b, 0, h))
    stacked = (k_all.shape[0], batch, seq, width)
    out = pl.pallas_call(
        functools.partial(_sb_prompt_kernel, tk=tk, scale2=dh ** -0.5 * LOG2E),
        grid=(batch, SB_HEADS),
        in_specs=[
            pl.BlockSpec(memory_space=pltpu.SMEM),
            pl.BlockSpec((None, seq, dh), lambda b, h: (b, 0, h)),
            kv_spec,
            kv_spec,
        ],
        out_specs=pl.BlockSpec((None, seq, dh), lambda b, h: (b, 0, h)),
        out_shape=jax.ShapeDtypeStruct((batch, seq, width), BF16),
        scratch_shapes=[
            pltpu.VMEM((seq, dh), BF16), pltpu.VMEM((seq, dh), BF16), pltpu.VMEM((seq, dh), BF16),
            pltpu.VMEM((tq, dh), F32), pltpu.VMEM((tq, 1), F32),
        ],
        compiler_params=_params("arbitrary", "arbitrary"),
        name="sb_prompt",
    )(bias, q.reshape(batch, seq, width), k_all.reshape(stacked), v_all.reshape(stacked))
    return out.reshape(m, width)


def _sb_sample_page(q, bias2, k_ref, v_ref, carry, valid, tri, *, scale2):
    n_pairs = SUBLANES
    n_lanes = tri.shape[0]
    rows = q.shape[0] // n_pairs
    z = jnp.concatenate(
        [_dot_nt(q[p * rows:(p + 1) * rows], k_ref[pl.ds(p, n_lanes, stride=n_pairs), :].astype(BF16))
         for p in range(n_pairs)], axis=0) * scale2 + bias2
    sp = _softplus2(z)
    log_not = jnp.where(valid, -sp, 0.0)
    between = _dot(log_not.astype(BF16), tri) + carry
    a = jnp.where(valid, jnp.exp2((z - sp) + between), 0.0).astype(BF16)
    out = jnp.concatenate(
        [_dot(a[p * rows:(p + 1) * rows], v_ref[pl.ds(p, n_lanes, stride=n_pairs), :].astype(BF16))
         for p in range(n_pairs)], axis=0)
    return out, carry + jnp.sum(log_not, axis=1, keepdims=True)


def _sb_sample_kernel(pt_ref, q_ref, bias_ref, knew_ref, vnew_ref, *refs, n_pages_step, n_new, scale2):
    k_refs = refs[:n_pages_step]
    v_refs = refs[n_pages_step:2 * n_pages_step]
    o_ref, acc_ref, carry_ref = refs[2 * n_pages_step:]
    s = pl.program_id(1)
    q = q_ref[...]
    bias2 = bias_ref[...] * LOG2E
    n_rows = q.shape[0]

    def masks(n_lanes):
        r = lax.broadcasted_iota(jnp.int32, (n_rows, n_lanes), 0)
        c = lax.broadcasted_iota(jnp.int32, (n_rows, n_lanes), 1)
        same_head = (r // n_new) % 2 == c % 2
        tr = lax.broadcasted_iota(jnp.int32, (n_lanes, n_lanes), 0)
        tc = lax.broadcasted_iota(jnp.int32, (n_lanes, n_lanes), 1)
        tri = jnp.where((tr // 2 > tc // 2) & (tr % 2 == tc % 2), 1.0, 0.0).astype(BF16)
        return r, c, same_head, tri

    @pl.when(s == 0)
    def _():
        r, c, same_head, tri = masks(knew_ref.shape[0] // SUBLANES)
        valid = same_head & (c // 2 < r % n_new)
        out, carry = _sb_sample_page(q, bias2, knew_ref, vnew_ref, jnp.zeros((n_rows, 1), F32), valid, tri,
                                     scale2=scale2)
        acc_ref[...] = out
        carry_ref[...] = carry

    _, _, same_head, tri = masks(k_refs[0].shape[0] // SUBLANES)
    carry = carry_ref[...]
    acc = acc_ref[...]
    for g in range(n_pages_step):
        out, carry = _sb_sample_page(q, bias2, k_refs[g], v_refs[g], carry, same_head, tri, scale2=scale2)
        acc = acc + out
    acc_ref[...] = acc
    carry_ref[...] = carry

    @pl.when(s == pl.num_programs(1) - 1)
    def _():
        dh = q.shape[1]
        for p in range(SUBLANES):
            for e in range(2):
                h = e * SUBLANES + p
                src = (p * 2 + e) * n_new
                o_ref[:, h * dh:(h + 1) * dh] = acc_ref[src:src + n_new, :].astype(o_ref.dtype)


def _sb_sample(qkv, bias, cache_k, cache_v, page_table, batch, n_new, layer_j):
    _, m, width = qkv.shape
    dh = width // SB_HEADS
    n_layers, n_pool, page = cache_k.shape[:3]
    n_pages = page_table.shape[1]
    g = PAGES_PER_STEP
    assert n_pages % g == 0 and SB_HEADS == 2 * SUBLANES and dh == LANES
    n_rows = SB_HEADS * n_new
    pad_tok = LANES // 2 - n_new
    q = qkv[0].reshape(batch, n_new, 2, SUBLANES, dh).transpose(0, 3, 2, 1, 4).reshape(batch, n_rows, dh)
    bias_rows = jnp.repeat(bias.reshape(2, SUBLANES).T.reshape(-1), n_new).reshape(n_rows, 1)

    def new_rows(part):
        x = qkv[part].reshape(batch, n_new, SB_HEADS, dh)
        return jnp.pad(x, ((0, 0), (0, pad_tok), (0, 0), (0, 0))).reshape(batch, (n_new + pad_tok) * SB_HEADS, dh)

    ck = cache_k.reshape(n_layers, n_pool, page * SB_HEADS, dh)
    cv = cache_v.reshape(n_layers, n_pool, page * SB_HEADS, dh)

    def page_spec(gi):
        return pl.BlockSpec((None, None, page * SB_HEADS, dh),
                            lambda b, s, pt: (layer_j, pt[b, n_pages - 1 - (s * g + gi)], 0, 0))

    new_spec = pl.BlockSpec((None, (n_new + pad_tok) * SB_HEADS, dh), lambda b, s, pt: (b, 0, 0))
    grid_spec = pltpu.PrefetchScalarGridSpec(
        num_scalar_prefetch=1,
        grid=(batch, n_pages // g),
        in_specs=[
            pl.BlockSpec((None, n_rows, dh), lambda b, s, pt: (b, 0, 0)),
            pl.BlockSpec((n_rows, 1), lambda b, s, pt: (0, 0)),
            new_spec,
            new_spec,
        ] + [page_spec(gi) for gi in range(g)] * 2,
        out_specs=pl.BlockSpec((None, n_new, width), lambda b, s, pt: (b, 0, 0)),
        scratch_shapes=[pltpu.VMEM((n_rows, dh), F32), pltpu.VMEM((n_rows, 1), F32)],
    )
    out = pl.pallas_call(
        functools.partial(_sb_sample_kernel, n_pages_step=g, n_new=n_new, scale2=dh ** -0.5 * LOG2E),
        grid_spec=grid_spec,
        out_shape=jax.ShapeDtypeStruct((batch, n_new, width), BF16),
        compiler_params=_params("arbitrary", "arbitrary"),
        name="sb_sample",
    )(page_table, q.astype(BF16), bias_rows, new_rows(1), new_rows(2), *([ck] * g), *([cv] * g))
    return out.reshape(m, width)


def _gla_gate_kernel(h_ref, wr_ref, wg_ref, b_ref, o_ref):
    glr = _dot(h_ref[...], wr_ref[...].astype(BF16))
    x = _dot(glr.astype(BF16), wg_ref[...].astype(BF16)) + b_ref[...]
    o_ref[...] = -_softplus(-x) / GLA_TAU


def _gla_gate(st, h, w_in, w_gate, b_gate, j):
    m, d = h.shape
    dk = w_gate.shape[-1]
    rank = w_gate.shape[1]
    w_r = jnp.pad(w_in[j][:, w_in.shape[-1] - rank:], ((0, 0), (0, LANES - rank)))
    w_g = jnp.pad(w_gate[j], ((0, LANES - rank), (0, 0)))
    tm = st.norm_row_tile
    return pl.pallas_call(
        _gla_gate_kernel,
        grid=(m // tm,),
        in_specs=[
            pl.BlockSpec((tm, d), lambda i: (i, 0)),
            pl.BlockSpec((d, LANES), lambda i: (0, 0)),
            pl.BlockSpec((LANES, dk), lambda i: (0, 0)),
            pl.BlockSpec((1, dk), lambda i: (0, 0)),
        ],
        out_specs=pl.BlockSpec((tm, dk), lambda i: (i, 0)),
        out_shape=jax.ShapeDtypeStruct((m, dk), F32),
        compiler_params=_params("arbitrary"),
        name="gla_gate",
    )(h, w_r, w_g, b_gate[j].reshape(1, dk))


def _gla_kernel(q_ref, k_ref, v_ref, r_ref, g_ref, ng_ref, s0_ref, o_ref, s_ref, *, chunk, n_chunks, scale):
    s_ref[...] = s0_ref[...]
    row = lax.broadcasted_iota(jnp.int32, (chunk, chunk), 0)
    col = lax.broadcasted_iota(jnp.int32, (chunk, chunk), 1)
    keep = col <= row
    lower = jnp.where(keep, 1.0, 0.0).astype(BF16)
    ones = jnp.ones((chunk, LANES), BF16)
    ng = ng_ref[...]
    dv = v_ref.shape[-1]

    def body(c, _):
        rows = slice(None) if n_chunks == 1 else pl.ds(pl.multiple_of(c * chunk, chunk), chunk)
        g = g_ref[rows, :]
        g_hi, g_lo = _split_bf16(g)
        bcum = _dot(lower, g_hi) + _dot(lower, g_lo)
        b_last = bcum[chunk - 1:chunk, :]
        q = q_ref[rows, :] * scale
        k = k_ref[rows, :]
        v = v_ref[rows, :].astype(BF16)
        qe = (q * jnp.exp(bcum)).astype(BF16)
        ke = (k * jnp.exp(-bcum)).astype(BF16)
        kl = (k * jnp.exp(b_last - bcum)).astype(BF16)
        state = s_ref[...]
        scores = jnp.where(keep, _dot_nt(qe, ke), 0.0)
        o = _dot(qe, state.astype(BF16)) + _dot(scores.astype(BF16), v)
        total = _dot_tn(g_hi, ones) + _dot_tn(g_lo, ones)
        decay = jnp.exp(total)
        s_ref[...] = jnp.concatenate([decay] * (dv // LANES), axis=1) * state + _dot_tn(kl, v)
        on = o * lax.rsqrt(jnp.mean(o * o, axis=-1, keepdims=True) + RMS_EPS) * ng
        r = r_ref[rows, :]
        o_ref[rows, :] = (on * (r * _sigmoid(r))).astype(o_ref.dtype)
        return 0

    if n_chunks == 1:
        body(0, 0)
    else:
        unroll = 2 if n_chunks % 2 == 0 else 1

        def body_unrolled(c, _):
            for u in range(unroll):
                body(c * unroll + u, 0)
            return 0

        lax.fori_loop(0, n_chunks // unroll, body_unrolled, 0)


def _gla(proj, g, norm_g, s0, batch, seq):
    _, m, d = proj.shape
    dk = g.shape[1]
    dkh, dvh = dk // GLA_HEADS, d // GLA_HEADS
    chunk = min(GLA_CHUNK, seq)
    p4 = proj.reshape(3, batch, seq, d)
    g3 = g.reshape(batch, seq, dk)
    nkh = dk // dkh
    out, state = pl.pallas_call(
        functools.partial(_gla_kernel, chunk=chunk, n_chunks=seq // chunk, scale=dkh ** -0.5),
        grid=(batch, GLA_HEADS),
        in_specs=[
            pl.BlockSpec((None, None, seq, dkh), lambda b, h: (0, b, 0, h)),
            pl.BlockSpec((None, None, seq, dkh), lambda b, h: (0, b, 0, nkh + h)),
            pl.BlockSpec((None, None, seq, dvh), lambda b, h: (1, b, 0, h)),
            pl.BlockSpec((None, None, seq, dvh), lambda b, h: (2, b, 0, h)),
            pl.BlockSpec((None, seq, dkh), lambda b, h: (b, 0, h)),
            pl.BlockSpec((1, dvh), lambda b, h: (0, 0)),
            pl.BlockSpec((None, None, dkh, dvh), lambda b, h: (b, h, 0, 0)),
        ],
        out_specs=[
            pl.BlockSpec((None, seq, dvh), lambda b, h: (b, 0, h)),
            pl.BlockSpec((None, None, dkh, dvh), lambda b, h: (b, h, 0, 0)),
        ],
        out_shape=[
            jax.ShapeDtypeStruct((batch, seq, d), BF16),
            jax.ShapeDtypeStruct((batch, GLA_HEADS, dkh, dvh), F32),
        ],
        compiler_params=_params("arbitrary", "arbitrary"),
        name="gla",
    )(p4, p4, p4, p4, g3, norm_g.reshape(1, dvh), s0)
    return out.reshape(m, d), state


def _conv_kernel(gb_ref, gc_ref, u_ref, w_ref, buf_ref, o_ref, tail_ref, z_ref, *, tt, n_prev):
    i = pl.program_id(1)

    @pl.when(i == 0)
    def _():
        z_ref[SUBLANES - n_prev:SUBLANES, :] = buf_ref[...]

    z = gc_ref[...] * u_ref[...]
    z_ref[SUBLANES:SUBLANES + tt, :] = z
    conv = w_ref[n_prev:n_prev + 1, :] * z
    for j in range(n_prev):
        conv = conv + w_ref[j:j + 1, :] * z_ref[SUBLANES - n_prev + j:SUBLANES - n_prev + j + tt, :]
    o_ref[...] = (gb_ref[...] * conv).astype(o_ref.dtype)
    last = z_ref[tt:tt + SUBLANES, :]
    z_ref[0:SUBLANES, :] = last
    tail_ref[...] = last[SUBLANES - n_prev:, :]


def _conv(proj, w_conv, buf, batch, seq):
    _, m, d = proj.shape
    n_prev = CONV_WIDTH - 1
    tt = min(seq, CONV_ROW_TILE)
    p4 = proj.reshape(3, batch, seq, d)
    part = lambda p: pl.BlockSpec((None, None, tt, d), lambda b, i: (p, b, i, 0))
    out, tail = pl.pallas_call(
        functools.partial(_conv_kernel, tt=tt, n_prev=n_prev),
        grid=(batch, seq // tt),
        in_specs=[
            part(0), part(1), part(2),
            pl.BlockSpec((CONV_WIDTH, d), lambda b, i: (0, 0)),
            pl.BlockSpec((None, n_prev, d), lambda b, i: (b, 0, 0)),
        ],
        out_specs=[
            pl.BlockSpec((None, tt, d), lambda b, i: (b, i, 0)),
            pl.BlockSpec((None, n_prev, d), lambda b, i: (b, 0, 0)),
        ],
        out_shape=[
            jax.ShapeDtypeStruct((batch, seq, d), BF16),
            jax.ShapeDtypeStruct((batch, n_prev, d), F32),
        ],
        scratch_shapes=[pltpu.VMEM((tt + SUBLANES, d), F32)],
        compiler_params=_params("arbitrary", "arbitrary"),
        name="short_conv",
    )(p4, p4, p4, w_conv, buf)
    return out.reshape(m, d), tail


def kernel(x_prompt, x_sample, c_prompt, c_sample, cache_sb_k, cache_sb_v, page_table, state_gla, state_conv,
           norm_g, w_mod, b_mod, ffn_w_in, ffn_w_out, sb_w_in, sb_w_out, sb_bias, gla_w_in, gla_w_gate, gla_b_gate,
           gla_norm_g, gla_w_out, conv_w_in, conv_w, conv_w_out, final_norm_g):
    batch, seq, d = x_prompt.shape
    dec_b, dec_seq, _ = x_sample.shape
    depth = norm_g.shape[0]
    dh = d // SB_HEADS

    n_seq = batch + dec_b
    pad = (-n_seq) % SUBLANES
    c_all = jnp.concatenate([c_prompt, c_sample, jnp.zeros((pad, d), F32)], axis=0)
    mods = _modulation(c_all, w_mod, b_mod)
    mod_p = mods[:, :batch].reshape(depth, batch, N_MOD, 1, d)
    mod_s = jnp.repeat(mods[:, batch:n_seq], dec_seq, axis=1).reshape(depth, 1, dec_b * dec_seq, N_MOD, d)
    mod_s = mod_s.transpose(0, 1, 3, 2, 4)

    rows_s = dec_b * dec_seq
    st_p = _Stream(batch, seq, mod_p, ROW_TILE, NORM_ROW_TILE)
    st_s = _Stream(dec_b, dec_seq, mod_s, rows_s, rows_s)
    xp = x_prompt.reshape(batch * seq, d)
    xs = x_sample.reshape(rows_s, d)

    def ffn_half(xp, xs, l, sub, which):
        hp = _norm_mod(st_p, xp, norm_g[l, sub], l, sub)
        hs = _norm_mod(st_s, xs, norm_g[l, sub], l, sub)
        up, us = _ffn_in(st_p, hp, hs, ffn_w_in, (l, which))
        return _out_proj(st_p, st_s, up, us, ffn_w_out, (l, which), xp, xs, l, sub, 0.5)

    n_sb = sb_w_in.shape[0]
    k_all = v_all = None
    sbk_s, sbv_s = [], []
    gla_p, gla_s, conv_p, conv_s = [], [], [], []

    for l in range(depth):
        xp, xs = ffn_half(xp, xs, l, 0, 0)
        hp = _norm_mod(st_p, xp, norm_g[l, 1], l, 1)
        hs = _norm_mod(st_s, xs, norm_g[l, 1], l, 1)
        kind, j = l % N_MIXERS, l // N_MIXERS
        if kind == 0:
            q_p, k_all, v_all, qkv_s = _qkv_proj(st_p, hp, hs, sb_w_in, j, n_sb, k_all, v_all)
            op = _sb_prompt(q_p, k_all, v_all, sb_bias[j], batch, seq, j)
            os_ = _sb_sample(qkv_s, sb_bias[j], cache_sb_k, cache_sb_v, page_table, dec_b, dec_seq, j)
            sbk_s.append(qkv_s[1].reshape(dec_b, dec_seq, SB_HEADS, dh))
            sbv_s.append(qkv_s[2].reshape(dec_b, dec_seq, SB_HEADS, dh))
            w_out, widx = sb_w_out, (j,)
        elif kind == 1:
            pr_p, pr_s = _in_proj(st_p, hp, hs, gla_w_in, (j,), 3)
            g_p = _gla_gate(st_p, hp, gla_w_in, gla_w_gate, gla_b_gate, j)
            g_s = _gla_gate(st_s, hs, gla_w_in, gla_w_gate, gla_b_gate, j)
            s0 = jnp.zeros((batch,) + state_gla.shape[2:], F32)
            op, sp = _gla(pr_p, g_p, gla_norm_g[j], s0, batch, seq)
            os_, ss = _gla(pr_s, g_s, gla_norm_g[j], state_gla[j], dec_b, dec_seq)
            gla_p.append(sp)
            gla_s.append(ss)
            w_out, widx = gla_w_out, (j,)
        else:
            pr_p, pr_s = _in_proj(st_p, hp, hs, conv_w_in, (j,), 3)
            b0 = jnp.zeros((batch, CONV_WIDTH - 1, d), F32)
            op, bp = _conv(pr_p, conv_w[j], b0, batch, seq)
            os_, bs = _conv(pr_s, conv_w[j], state_conv[j], dec_b, dec_seq)
            conv_p.append(bp)
            conv_s.append(bs)
            w_out, widx = conv_w_out, (j,)
        xp, xs = _out_proj(st_p, st_s, op, os_, w_out, widx, xp, xs, l, 1, 1.0)
        xp, xs = ffn_half(xp, xs, l, 2, 1)

    y_prompt = _final_norm(st_p, xp, final_norm_g).reshape(batch, seq, d)
    y_sample = _final_norm(st_s, xs, final_norm_g).reshape(dec_b, dec_seq, d)
    kv_shape = (n_sb, batch, seq, SB_HEADS, dh)
    return (y_prompt, y_sample, k_all.reshape(kv_shape), v_all.reshape(kv_shape), jnp.stack(sbk_s), jnp.stack(sbv_s),
            jnp.stack(gla_p), jnp.stack(gla_s), jnp.stack(conv_p), jnp.stack(conv_s))
```

```python
import functools

import jax
import jax.numpy as jnp
from jax import lax
from jax.experimental import pallas as pl
from jax.experimental.pallas import tpu as pltpu

F32 = jnp.float32
BF16 = jnp.bfloat16

N_MIXERS = 3
N_MOD = 9
SB_HEADS = 16
GLA_HEADS = 4
GLA_RANK = 16
GLA_TAU = 16.0
GLA_CHUNK = 64
CONV_WIDTH = 3
RMS_EPS = 1e-6
LOG2E = 1.4426950408889634

LANES = 128
SUBLANES = 8
VMEM_BYTES_V7X = 64 * 1024 * 1024
VMEM_LIMIT = VMEM_BYTES_V7X - 8 * 1024 * 1024

ROW_TILE = 1024
COL_TILE = 1024
QKV_COL_TILE = 512
FFN_COL_TILE = 512
CAST_ROWS = 512
OUT_ROW_TILE_DEEP = 256
OUT_ROW_TILE = 512
NORM_ROW_TILE = 1024
CONV_ROW_TILE = 512
MOD_COL_TILE = 1024
SB_K_TILE = 256
PAGES_PER_STEP = 8


def _params(*sem):
    return pltpu.CompilerParams(dimension_semantics=sem, vmem_limit_bytes=VMEM_LIMIT)


def _sigmoid(x):
    return 1.0 / (1.0 + jnp.exp(-x))


def _softplus(z):
    return jnp.maximum(z, 0.0) + jnp.log1p(jnp.exp(-jnp.abs(z)))


def _split_bf16(x):
    hi = x.astype(BF16)
    lo = (x - hi.astype(F32)).astype(BF16)
    return hi, lo


def _dot(a, b):
    return jnp.dot(a, b, preferred_element_type=F32)


def _dot_nt(a, b):
    return lax.dot_general(a, b, (((1,), (1,)), ((), ())), preferred_element_type=F32)


def _dot_tn(a, b):
    return lax.dot_general(a, b, (((0,), (0,)), ((), ())), preferred_element_type=F32)


def _mod_kernel(c_ref, w_ref, b_ref, o_ref):
    c = c_ref[...]
    s = (c * _sigmoid(c)).astype(BF16)
    o_ref[...] = _dot(s, w_ref[...].astype(BF16)) + b_ref[...]


def _modulation(c_all, w_mod, b_mod):
    depth, d, n = w_mod.shape
    rows = c_all.shape[0]
    tn = MOD_COL_TILE
    return pl.pallas_call(
        _mod_kernel,
        grid=(depth, n // tn),
        in_specs=[
            pl.BlockSpec((rows, d), lambda l, j: (0, 0)),
            pl.BlockSpec((None, d, tn), lambda l, j: (l, 0, j)),
            pl.BlockSpec((None, 1, tn), lambda l, j: (l, 0, j)),
        ],
        out_specs=pl.BlockSpec((None, rows, tn), lambda l, j: (l, 0, j)),
        out_shape=jax.ShapeDtypeStruct((depth, rows, n), F32),
        compiler_params=_params("arbitrary", "arbitrary"),
        name="modulation",
    )(c_all, w_mod, b_mod.reshape(depth, 1, n))


class _Stream:
    def __init__(self, batch, seq, mod, row_tile, norm_row_tile):
        self.batch, self.seq, self.mod = batch, seq, mod
        self.rows = batch * seq
        self.row_tile, self.norm_row_tile = row_tile, norm_row_tile
        self.per_tile_mod = mod.shape[3] != 1

    def mod_spec(self, layer, chunk, tm, width, col_of, row_of):
        r = self.mod.shape[3]
        if self.per_tile_mod:
            assert r == tm
            group = lambda *g: 0
        else:
            tiles_per_seq = self.seq // tm
            group = lambda *g: row_of(*g) // tiles_per_seq
        return pl.BlockSpec((None, None, None, r, width),
                            lambda *g: (layer, group(*g), chunk, 0, col_of(*g)))


def _norm_mod_kernel(x_ref, g_ref, shift_ref, scale_ref, o_ref):
    x = x_ref[...]
    y = x * lax.rsqrt(jnp.mean(x * x, axis=-1, keepdims=True) + RMS_EPS) * g_ref[...]
    o_ref[...] = (y * (1.0 + scale_ref[...]) + shift_ref[...]).astype(o_ref.dtype)


def _norm_mod(st, x, g, layer, sub):
    m, d = x.shape
    tm = st.norm_row_tile
    row_of = lambda i: i
    col_of = lambda i: 0
    return pl.pallas_call(
        _norm_mod_kernel,
        grid=(m // tm,),
        in_specs=[
            pl.BlockSpec((tm, d), lambda i: (i, 0)),
            pl.BlockSpec((1, d), lambda i: (0, 0)),
            st.mod_spec(layer, 3 * sub, tm, d, col_of, row_of),
            st.mod_spec(layer, 3 * sub + 1, tm, d, col_of, row_of),
        ],
        out_specs=pl.BlockSpec((tm, d), lambda i: (i, 0)),
        out_shape=jax.ShapeDtypeStruct((m, d), BF16),
        compiler_params=_params("arbitrary"),
        name="norm_mod",
    )(x, g.reshape(1, d), st.mod, st.mod)


def _cast_chunk_specs(w_out, layer_j, n_steps):
    k_out, d = w_out.shape[-2:]
    n_chunks = k_out // CAST_ROWS
    assert k_out % CAST_ROWS == 0 and n_chunks <= n_steps
    chunk = lambda j: jnp.minimum(j, n_chunks - 1)
    in_spec = pl.BlockSpec((None, CAST_ROWS, d), lambda j, i: (layer_j, chunk(j), 0))
    out_spec = pl.BlockSpec((CAST_ROWS, d), lambda j, i: (chunk(j), 0))
    return in_spec, out_spec, jax.ShapeDtypeStruct((k_out, d), BF16), n_chunks


def _cast_chunk(wo_ref, wob_ref, n_chunks):
    @pl.when((pl.program_id(1) == 0) & (pl.program_id(0) < n_chunks))
    def _():
        wob_ref[...] = wo_ref[...].astype(BF16)


def _mm_kernel(a_ref, as_ref, w_ref, wo_ref, o_ref, os_ref, wob_ref, wb_ref, *, n_chunks):
    @pl.when(pl.program_id(1) == 0)
    def _():
        wb_ref[...] = w_ref[...].astype(BF16)
        os_ref[...] = _dot(as_ref[...], wb_ref[...])

    _cast_chunk(wo_ref, wob_ref, n_chunks)
    o_ref[...] = _dot(a_ref[...], wb_ref[...])


def _in_proj(st, a, a_s, w, w_out, layer_j, n_parts):
    m, k = a.shape
    ms = a_s.shape[0]
    tm, tn = st.row_tile, COL_TILE
    tpp = k // tn
    wo_in, wo_out, wo_shape, n_chunks = _cast_chunk_specs(w_out, layer_j, n_parts * tpp)
    return pl.pallas_call(
        functools.partial(_mm_kernel, n_chunks=n_chunks),
        grid=(n_parts * tpp, m // tm),
        in_specs=[
            pl.BlockSpec((tm, k), lambda j, i: (i, 0)),
            pl.BlockSpec((ms, k), lambda j, i: (0, 0)),
            pl.BlockSpec((None, k, tn), lambda j, i: (layer_j, 0, j)),
            wo_in,
        ],
        out_specs=[
            pl.BlockSpec((None, tm, tn), lambda j, i: (j // tpp, i, j % tpp)),
            pl.BlockSpec((None, ms, tn), lambda j, i: (j // tpp, 0, j % tpp)),
            wo_out,
        ],
        out_shape=[jax.ShapeDtypeStruct((n_parts, m, k), F32), jax.ShapeDtypeStruct((n_parts, ms, k), F32), wo_shape],
        scratch_shapes=[pltpu.VMEM((k, tn), BF16)],
        compiler_params=_params("arbitrary", "arbitrary"),
        name="in_proj",
    )(a, a_s, w, w_out)


def _qkv_kernel(a_ref, as_ref, w_ref, wo_ref, kin_ref, vin_ref, q_ref, k_ref, v_ref, os_ref, wob_ref, wb_ref,
                *, tpp, n_chunks):
    del kin_ref, vin_ref
    part = pl.program_id(0) // tpp

    @pl.when(pl.program_id(1) == 0)
    def _():
        wb_ref[...] = w_ref[...].astype(BF16)
        os_ref[...] = _dot(as_ref[...], wb_ref[...])

    _cast_chunk(wo_ref, wob_ref, n_chunks)
    y = _dot(a_ref[...], wb_ref[...])
    for p, ref in enumerate((q_ref, k_ref, v_ref)):
        @pl.when(part == p)
        def _(ref=ref):
            ref[...] = y


def _qkv_proj(st, a, a_s, w, w_out, layer_j, k_all, v_all):
    m, k = a.shape
    ms = a_s.shape[0]
    tm, tn = st.row_tile, QKV_COL_TILE
    tpp = k // tn
    ni = m // tm
    wo_in, wo_out, wo_shape, n_chunks = _cast_chunk_specs(w_out, layer_j, 3 * tpp)

    def rows(j, i, part):
        return jnp.where(j < part * tpp, 0, jnp.where(j < (part + 1) * tpp, i, ni - 1))

    def cols(j, part):
        return jnp.clip(j - part * tpp, 0, tpp - 1)

    stacked = jax.ShapeDtypeStruct(k_all.shape, F32)
    any_spec = pl.BlockSpec(memory_space=pl.ANY)
    return pl.pallas_call(
        functools.partial(_qkv_kernel, tpp=tpp, n_chunks=n_chunks),
        grid=(3 * tpp, ni),
        in_specs=[
            pl.BlockSpec((tm, k), lambda j, i: (i, 0)),
            pl.BlockSpec((ms, k), lambda j, i: (0, 0)),
            pl.BlockSpec((None, k, tn), lambda j, i: (layer_j, 0, j)),
            wo_in,
            any_spec,
            any_spec,
        ],
        out_specs=[
            pl.BlockSpec((tm, tn), lambda j, i: (rows(j, i, 0), cols(j, 0))),
            pl.BlockSpec((None, tm, tn), lambda j, i: (layer_j, rows(j, i, 1), cols(j, 1))),
            pl.BlockSpec((None, tm, tn), lambda j, i: (layer_j, rows(j, i, 2), cols(j, 2))),
            pl.BlockSpec((None, ms, tn), lambda j, i: (j // tpp, 0, j % tpp)),
            wo_out,
        ],
        out_shape=[jax.ShapeDtypeStruct((m, k), F32), stacked, stacked, jax.ShapeDtypeStruct((3, ms, k), F32),
                   wo_shape],
        scratch_shapes=[pltpu.VMEM((k, tn), BF16)],
        input_output_aliases={4: 1, 5: 2},
        compiler_params=_params("arbitrary", "arbitrary"),
        name="qkv_proj",
    )(a, a_s, w, w_out, k_all, v_all)


def _swiglu(h, wa, wb):
    a = _dot(h, wa)
    return (a * _sigmoid(a)) * _dot(h, wb)


def _ffn_in_kernel(a_ref, as_ref, wa_ref, wb_ref, wo_ref, o_ref, os_ref, wob_ref, wab_ref, wbb_ref):
    @pl.when(pl.program_id(1) == 0)
    def _():
        wab_ref[...] = wa_ref[...].astype(BF16)
        wbb_ref[...] = wb_ref[...].astype(BF16)
        wob_ref[...] = wo_ref[...].astype(BF16)
        os_ref[...] = _swiglu(as_ref[...], wab_ref[...], wbb_ref[...]).astype(os_ref.dtype)

    o_ref[...] = _swiglu(a_ref[...], wab_ref[...], wbb_ref[...]).astype(o_ref.dtype)


def _ffn_in(st, a, a_s, w, w_out, widx):
    m, k = a.shape
    ms = a_s.shape[0]
    hidden, d = w_out.shape[-2:]
    tm, tn = st.row_tile, FFN_COL_TILE
    nj = hidden // tn
    wspec = lambda off: pl.BlockSpec((None, None, k, tn), lambda j, i: widx + (0, j + off))
    return pl.pallas_call(
        _ffn_in_kernel,
        grid=(nj, m // tm),
        in_specs=[pl.BlockSpec((tm, k), lambda j, i: (i, 0)), pl.BlockSpec((ms, k), lambda j, i: (0, 0)),
                  wspec(0), wspec(nj), pl.BlockSpec((None, None, tn, d), lambda j, i: widx + (j, 0))],
        out_specs=[pl.BlockSpec((tm, tn), lambda j, i: (i, j)), pl.BlockSpec((ms, tn), lambda j, i: (0, j)),
                   pl.BlockSpec((tn, d), lambda j, i: (j, 0))],
        out_shape=[jax.ShapeDtypeStruct((m, hidden), BF16), jax.ShapeDtypeStruct((ms, hidden), BF16),
                   jax.ShapeDtypeStruct((hidden, d), BF16)],
        scratch_shapes=[pltpu.VMEM((k, tn), BF16), pltpu.VMEM((k, tn), BF16)],
        compiler_params=_params("arbitrary", "arbitrary"),
        name="ffn_in",
    )(a, a_s, w, w, w_out)


def _out_norm_kernel(u_ref, us_ref, w_ref, x_ref, xs_ref, gate_ref, gates_ref, g_ref, shift_ref, shifts_ref,
                     scale_ref, scales_ref, o_ref, os_ref, h_ref, hs_ref, *, coef, final):
    def tile(u_ref, x_ref, gate_ref, shift_ref, scale_ref, o_ref, h_ref):
        gate = gate_ref[...]
        if coef != 1.0:
            gate = coef * gate
        x = x_ref[...] + gate * _dot(u_ref[...], w_ref[...])
        o_ref[...] = x
        y = x * lax.rsqrt(jnp.mean(x * x, axis=-1, keepdims=True) + RMS_EPS) * g_ref[...]
        if not final:
            y = y * (1.0 + scale_ref[...]) + shift_ref[...]
        h_ref[...] = y.astype(h_ref.dtype)

    @pl.when(pl.program_id(0) == 0)
    def _():
        tile(us_ref, xs_ref, gates_ref, shifts_ref, scales_ref, os_ref, hs_ref)

    tile(u_ref, x_ref, gate_ref, shift_ref, scale_ref, o_ref, h_ref)


def _out_norm(st, st_s, u, u_s, w_bf16, x, x_s, layer, sub, coef, g_next, next_layer, next_sub):
    m, k = u.shape
    ms = u_s.shape[0]
    d = x.shape[1]
    final = next_layer is None
    tm = OUT_ROW_TILE if k <= d else OUT_ROW_TILE_DEEP
    row_of, col_of, zero = (lambda i: i), (lambda i: 0), (lambda i: 0)
    nl, ns = (layer, sub) if final else (next_layer, next_sub)

    def mods(chunk_layer, chunk):
        return [st.mod_spec(chunk_layer, chunk, tm, d, col_of, row_of),
                st_s.mod_spec(chunk_layer, chunk, ms, d, col_of, zero)]

    h_dtype = F32 if final else BF16
    return pl.pallas_call(
        functools.partial(_out_norm_kernel, coef=coef, final=final),
        grid=(m // tm,),
        in_specs=[
            pl.BlockSpec((tm, k), lambda i: (i, 0)),
            pl.BlockSpec((ms, k), lambda i: (0, 0)),
            pl.BlockSpec((k, d), lambda i: (0, 0), pipeline_mode=pl.Buffered(1)),
            pl.BlockSpec((tm, d), lambda i: (i, 0)),
            pl.BlockSpec((ms, d), lambda i: (0, 0)),
        ] + mods(layer, 3 * sub + 2) + [pl.BlockSpec((1, d), lambda i: (0, 0))]
          + mods(nl, 3 * ns) + mods(nl, 3 * ns + 1),
        out_specs=[pl.BlockSpec((tm, d), lambda i: (i, 0)), pl.BlockSpec((ms, d), lambda i: (0, 0)),
                   pl.BlockSpec((tm, d), lambda i: (i, 0)), pl.BlockSpec((ms, d), lambda i: (0, 0))],
        out_shape=[jax.ShapeDtypeStruct((m, d), F32), jax.ShapeDtypeStruct((ms, d), F32),
                   jax.ShapeDtypeStruct((m, d), h_dtype), jax.ShapeDtypeStruct((ms, d), h_dtype)],
        compiler_params=_params("arbitrary"),
        name="out_norm",
    )(u, u_s, w_bf16, x, x_s, st.mod, st_s.mod, g_next.reshape(1, d), st.mod, st_s.mod, st.mod, st_s.mod)


def _softplus2(z2):
    neg_abs = pltpu.bitcast(pltpu.bitcast(z2, jnp.uint32) | jnp.uint32(0x80000000), F32)
    return jnp.maximum(z2, 0.0) + jnp.log(1.0 + jnp.exp2(neg_abs)) * LOG2E


def _sb_block(q, kb, vb, carry, mask, *, tri, bias2, scale2):
    z = _dot_nt(q, kb) * scale2 + bias2
    sp = _softplus2(z)
    log_not = -sp
    if mask is not None:
        log_not = jnp.where(mask, log_not, 0.0)
    between = _dot(log_not.astype(BF16), tri) + carry
    a = jnp.exp2((z - sp) + between)
    if mask is not None:
        a = jnp.where(mask, a, 0.0)
    new_carry = between[:, 0:1] + log_not[:, 0:1]
    return _dot(a.astype(BF16), vb), new_carry


def _sb_prompt_kernel(bias_ref, q_ref, k_ref, v_ref, o_ref, kb_ref, vb_ref, qb_ref, acc_ref, carry_ref,
                      *, tk, scale2):
    kb_ref[...] = k_ref[...].astype(BF16)
    vb_ref[...] = v_ref[...].astype(BF16)
    qb_ref[...] = q_ref[...].astype(BF16)
    tq = 2 * tk
    row = lax.broadcasted_iota(jnp.int32, (tk, tk), 0)
    col = lax.broadcasted_iota(jnp.int32, (tk, tk), 1)
    tri = jnp.where(row > col, 1.0, 0.0).astype(BF16)
    causal = col < row
    block = functools.partial(_sb_block, tri=tri, bias2=bias_ref[pl.program_id(1)] * LOG2E, scale2=scale2)

    def keys(j):
        start = pl.multiple_of(j * tk, tk)
        return kb_ref[pl.ds(start, tk), :], vb_ref[pl.ds(start, tk), :]

    zero = jnp.zeros((tk, 1), F32)

    def query_tile(i, _):
        q_lo = qb_ref[pl.ds(pl.multiple_of(i * tq, tq), tk), :]
        q_hi = qb_ref[pl.ds(pl.multiple_of(i * tq + tk, tk), tk), :]
        kb, vb = keys(2 * i + 1)
        out_b, carry_b = block(q_hi, kb, vb, zero, causal)
        kb, vb = keys(2 * i)
        out_a, carry_a = block(q_lo, kb, vb, zero, causal)
        out_c, carry_c = block(q_hi, kb, vb, carry_b, None)
        acc_ref[:tk, :] = out_a
        acc_ref[tk:, :] = out_b + out_c
        carry_ref[:tk, :] = carry_a
        carry_ref[tk:, :] = carry_c

        def body(t, _):
            p = i - 1 - t
            q = qb_ref[pl.ds(pl.multiple_of(i * tq, tq), tq), :]
            kb, vb = keys(2 * p + 1)
            out1, carry = block(q, kb, vb, carry_ref[...], None)
            kb, vb = keys(2 * p)
            out2, carry = block(q, kb, vb, carry, None)
            acc_ref[...] += out1 + out2
            carry_ref[...] = carry
            return 0

        lax.fori_loop(0, i, body, 0)
        o_ref[pl.ds(pl.multiple_of(i * tq, tq), tq), :] = acc_ref[...].astype(o_ref.dtype)
        return 0

    lax.fori_loop(0, q_ref.shape[0] // tq, query_tile, 0)


def _sb_prompt(q, k_all, v_all, bias, batch, seq, layer_j):
    m, width = q.shape
    dh = width // SB_HEADS
    tk = SB_K_TILE
    tq = 2 * tk
    assert seq % tq == 0
    kv_spec = pl.BlockSpec((None, None, seq, dh), lambda b, h: (layer_j, b, 0, h))
    stacked = (k_all.shape[0], batch, seq, width)
    out = pl.pallas_call(
        functools.partial(_sb_prompt_kernel, tk=tk, scale2=dh ** -0.5 * LOG2E),
        grid=(batch, SB_HEADS),
        in_specs=[
            pl.BlockSpec(memory_space=pltpu.SMEM),
            pl.BlockSpec((None, seq, dh), lambda b, h: (b, 0, h)),
            kv_spec,
            kv_spec,
        ],
        out_specs=pl.BlockSpec((None, seq, dh), lambda b, h: (b, 0, h)),
        out_shape=jax.ShapeDtypeStruct((batch, seq, width), BF16),
        scratch_shapes=[
            pltpu.VMEM((seq, dh), BF16), pltpu.VMEM((seq, dh), BF16), pltpu.VMEM((seq, dh), BF16),
            pltpu.VMEM((tq, dh), F32), pltpu.VMEM((tq, 1), F32),
        ],
        compiler_params=_params("arbitrary", "arbitrary"),
        name="sb_prompt",
    )(bias, q.reshape(batch, seq, width), k_all.reshape(stacked), v_all.reshape(stacked))
    return out.reshape(m, width)


def _sb_sample_page(q, bias2, k_ref, v_ref, carry, valid, tri, *, scale2):
    n_pairs = SUBLANES
    n_lanes = tri.shape[0]
    rows = q.shape[0] // n_pairs
    z = jnp.concatenate(
        [_dot_nt(q[p * rows:(p + 1) * rows], k_ref[pl.ds(p, n_lanes, stride=n_pairs), :].astype(BF16))
         for p in range(n_pairs)], axis=0) * scale2 + bias2
    sp = _softplus2(z)
    log_not = jnp.where(valid, -sp, 0.0)
    between = _dot(log_not.astype(BF16), tri) + carry
    a = jnp.where(valid, jnp.exp2((z - sp) + between), 0.0).astype(BF16)
    out = jnp.concatenate(
        [_dot(a[p * rows:(p + 1) * rows], v_ref[pl.ds(p, n_lanes, stride=n_pairs), :].astype(BF16))
         for p in range(n_pairs)], axis=0)
    return out, carry + jnp.sum(log_not, axis=1, keepdims=True)


def _sb_sample_kernel(pt_ref, q_ref, bias_ref, knew_ref, vnew_ref, *refs, n_pages_step, n_new, scale2):
    k_refs = refs[:n_pages_step]
    v_refs = refs[n_pages_step:2 * n_pages_step]
    o_ref, acc_ref, carry_ref = refs[2 * n_pages_step:]
    s = pl.program_id(1)
    q = q_ref[...]
    bias2 = bias_ref[...] * LOG2E
    n_rows = q.shape[0]

    def masks(n_lanes):
        r = lax.broadcasted_iota(jnp.int32, (n_rows, n_lanes), 0)
        c = lax.broadcasted_iota(jnp.int32, (n_rows, n_lanes), 1)
        same_head = (r // n_new) % 2 == c % 2
        tr = lax.broadcasted_iota(jnp.int32, (n_lanes, n_lanes), 0)
        tc = lax.broadcasted_iota(jnp.int32, (n_lanes, n_lanes), 1)
        tri = jnp.where((tr // 2 > tc // 2) & (tr % 2 == tc % 2), 1.0, 0.0).astype(BF16)
        return r, c, same_head, tri

    @pl.when(s == 0)
    def _():
        r, c, same_head, tri = masks(knew_ref.shape[0] // SUBLANES)
        valid = same_head & (c // 2 < r % n_new)
        out, carry = _sb_sample_page(q, bias2, knew_ref, vnew_ref, jnp.zeros((n_rows, 1), F32), valid, tri,
                                     scale2=scale2)
        acc_ref[...] = out
        carry_ref[...] = carry

    _, _, same_head, tri = masks(k_refs[0].shape[0] // SUBLANES)
    carry = carry_ref[...]
    acc = acc_ref[...]
    for g in range(n_pages_step):
        out, carry = _sb_sample_page(q, bias2, k_refs[g], v_refs[g], carry, same_head, tri, scale2=scale2)
        acc = acc + out
    acc_ref[...] = acc
    carry_ref[...] = carry

    @pl.when(s == pl.num_programs(1) - 1)
    def _():
        dh = q.shape[1]
        for p in range(SUBLANES):
            for e in range(2):
                h = e * SUBLANES + p
                src = (p * 2 + e) * n_new
                o_ref[:, h * dh:(h + 1) * dh] = acc_ref[src:src + n_new, :].astype(o_ref.dtype)


def _sb_sample(qkv, bias, cache_k, cache_v, page_table, batch, n_new, layer_j):
    _, m, width = qkv.shape
    dh = width // SB_HEADS
    n_layers, n_pool, page = cache_k.shape[:3]
    n_pages = page_table.shape[1]
    g = PAGES_PER_STEP
    assert n_pages % g == 0 and SB_HEADS == 2 * SUBLANES and dh == LANES
    n_rows = SB_HEADS * n_new
    pad_tok = LANES // 2 - n_new
    q = qkv[0].reshape(batch, n_new, 2, SUBLANES, dh).transpose(0, 3, 2, 1, 4).reshape(batch, n_rows, dh)
    bias_rows = jnp.repeat(bias.reshape(2, SUBLANES).T.reshape(-1), n_new).reshape(n_rows, 1)

    def new_rows(part):
        x = qkv[part].reshape(batch, n_new, SB_HEADS, dh)
        return jnp.pad(x, ((0, 0), (0, pad_tok), (0, 0), (0, 0))).reshape(batch, (n_new + pad_tok) * SB_HEADS, dh)

    ck = cache_k.reshape(n_layers, n_pool, page * SB_HEADS, dh)
    cv = cache_v.reshape(n_layers, n_pool, page * SB_HEADS, dh)

    def page_spec(gi):
        return pl.BlockSpec((None, None, page * SB_HEADS, dh),
                            lambda b, s, pt: (layer_j, pt[b, n_pages - 1 - (s * g + gi)], 0, 0))

    new_spec = pl.BlockSpec((None, (n_new + pad_tok) * SB_HEADS, dh), lambda b, s, pt: (b, 0, 0))
    grid_spec = pltpu.PrefetchScalarGridSpec(
        num_scalar_prefetch=1,
        grid=(batch, n_pages // g),
        in_specs=[
            pl.BlockSpec((None, n_rows, dh), lambda b, s, pt: (b, 0, 0)),
            pl.BlockSpec((n_rows, 1), lambda b, s, pt: (0, 0)),
            new_spec,
            new_spec,
        ] + [page_spec(gi) for gi in range(g)] * 2,
        out_specs=pl.BlockSpec((None, n_new, width), lambda b, s, pt: (b, 0, 0)),
        scratch_shapes=[pltpu.VMEM((n_rows, dh), F32), pltpu.VMEM((n_rows, 1), F32)],
    )
    out = pl.pallas_call(
        functools.partial(_sb_sample_kernel, n_pages_step=g, n_new=n_new, scale2=dh ** -0.5 * LOG2E),
        grid_spec=grid_spec,
        out_shape=jax.ShapeDtypeStruct((batch, n_new, width), BF16),
        compiler_params=_params("arbitrary", "arbitrary"),
        name="sb_sample",
    )(page_table, q.astype(BF16), bias_rows, new_rows(1), new_rows(2), *([ck] * g), *([cv] * g))
    return out.reshape(m, width)


def _gla_gate_kernel(h_ref, wr_ref, wg_ref, b_ref, o_ref):
    glr = _dot(h_ref[...], wr_ref[...].astype(BF16))
    x = _dot(glr.astype(BF16), wg_ref[...].astype(BF16)) + b_ref[...]
    o_ref[...] = -_softplus(-x) / GLA_TAU


def _gla_gate(st, h, w_in, w_gate, b_gate, j):
    m, d = h.shape
    dk = w_gate.shape[-1]
    rank = w_gate.shape[1]
    w_r = jnp.pad(w_in[j][:, w_in.shape[-1] - rank:], ((0, 0), (0, LANES - rank)))
    w_g = jnp.pad(w_gate[j], ((0, LANES - rank), (0, 0)))
    tm = st.norm_row_tile
    return pl.pallas_call(
        _gla_gate_kernel,
        grid=(m // tm,),
        in_specs=[
            pl.BlockSpec((tm, d), lambda i: (i, 0)),
            pl.BlockSpec((d, LANES), lambda i: (0, 0)),
            pl.BlockSpec((LANES, dk), lambda i: (0, 0)),
            pl.BlockSpec((1, dk), lambda i: (0, 0)),
        ],
        out_specs=pl.BlockSpec((tm, dk), lambda i: (i, 0)),
        out_shape=jax.ShapeDtypeStruct((m, dk), F32),
        compiler_params=_params("arbitrary"),
        name="gla_gate",
    )(h, w_r, w_g, b_gate[j].reshape(1, dk))


def _gla_kernel(q_ref, k_ref, v_ref, r_ref, g_ref, ng_ref, s0_ref, o_ref, s_ref, *, chunk, n_chunks, scale):
    s_ref[...] = s0_ref[...]
    row = lax.broadcasted_iota(jnp.int32, (chunk, chunk), 0)
    col = lax.broadcasted_iota(jnp.int32, (chunk, chunk), 1)
    keep = col <= row
    lower = jnp.where(keep, 1.0, 0.0).astype(BF16)
    ones = jnp.ones((chunk, LANES), BF16)
    ng = ng_ref[...]
    dv = v_ref.shape[-1]

    def body(c, _):
        rows = slice(None) if n_chunks == 1 else pl.ds(pl.multiple_of(c * chunk, chunk), chunk)
        g = g_ref[rows, :]
        g_hi, g_lo = _split_bf16(g)
        bcum = _dot(lower, g_hi) + _dot(lower, g_lo)
        b_last = bcum[chunk - 1:chunk, :]
        q = q_ref[rows, :] * scale
        k = k_ref[rows, :]
        v = v_ref[rows, :].astype(BF16)
        qe = (q * jnp.exp(bcum)).astype(BF16)
        ke = (k * jnp.exp(-bcum)).astype(BF16)
        kl = (k * jnp.exp(b_last - bcum)).astype(BF16)
        state = s_ref[...]
        scores = jnp.where(keep, _dot_nt(qe, ke), 0.0)
        o = _dot(qe, state.astype(BF16)) + _dot(scores.astype(BF16), v)
        total = _dot_tn(g_hi, ones) + _dot_tn(g_lo, ones)
        decay = jnp.exp(total)
        s_ref[...] = jnp.concatenate([decay] * (dv // LANES), axis=1) * state + _dot_tn(kl, v)
        on = o * lax.rsqrt(jnp.mean(o * o, axis=-1, keepdims=True) + RMS_EPS) * ng
        r = r_ref[rows, :]
        o_ref[rows, :] = (on * (r * _sigmoid(r))).astype(o_ref.dtype)
        return 0

    if n_chunks == 1:
        body(0, 0)
    else:
        unroll = 2 if n_chunks % 2 == 0 else 1

        def body_unrolled(c, _):
            for u in range(unroll):
                body(c * unroll + u, 0)
            return 0

        lax.fori_loop(0, n_chunks // unroll, body_unrolled, 0)


def _gla(proj, g, norm_g, s0, batch, seq):
    _, m, d = proj.shape
    dk = g.shape[1]
    dkh, dvh = dk // GLA_HEADS, d // GLA_HEADS
    chunk = min(GLA_CHUNK, seq)
    p4 = proj.reshape(3, batch, seq, d)
    g3 = g.reshape(batch, seq, dk)
    nkh = dk // dkh
    out, state = pl.pallas_call(
        functools.partial(_gla_kernel, chunk=chunk, n_chunks=seq // chunk, scale=dkh ** -0.5),
        grid=(batch, GLA_HEADS),
        in_specs=[
            pl.BlockSpec((None, None, seq, dkh), lambda b, h: (0, b, 0, h)),
            pl.BlockSpec((None, None, seq, dkh), lambda b, h: (0, b, 0, nkh + h)),
            pl.BlockSpec((None, None, seq, dvh), lambda b, h: (1, b, 0, h)),
            pl.BlockSpec((None, None, seq, dvh), lambda b, h: (2, b, 0, h)),
            pl.BlockSpec((None, seq, dkh), lambda b, h: (b, 0, h)),
            pl.BlockSpec((1, dvh), lambda b, h: (0, 0)),
            pl.BlockSpec((None, None, dkh, dvh), lambda b, h: (b, h, 0, 0)),
        ],
        out_specs=[
            pl.BlockSpec((None, seq, dvh), lambda b, h: (b, 0, h)),
            pl.BlockSpec((None, None, dkh, dvh), lambda b, h: (b, h, 0, 0)),
        ],
        out_shape=[
            jax.ShapeDtypeStruct((batch, seq, d), BF16),
            jax.ShapeDtypeStruct((batch, GLA_HEADS, dkh, dvh), F32),
        ],
        compiler_params=_params("arbitrary", "arbitrary"),
        name="gla",
    )(p4, p4, p4, p4, g3, norm_g.reshape(1, dvh), s0)
    return out.reshape(m, d), state


def _conv_kernel(gb_ref, gc_ref, u_ref, w_ref, buf_ref, o_ref, tail_ref, z_ref, *, tt, n_prev):
    i = pl.program_id(1)

    @pl.when(i == 0)
    def _():
        z_ref[SUBLANES - n_prev:SUBLANES, :] = buf_ref[...]

    z = gc_ref[...] * u_ref[...]
    z_ref[SUBLANES:SUBLANES + tt, :] = z
    conv = w_ref[n_prev:n_prev + 1, :] * z
    for j in range(n_prev):
        conv = conv + w_ref[j:j + 1, :] * z_ref[SUBLANES - n_prev + j:SUBLANES - n_prev + j + tt, :]
    o_ref[...] = (gb_ref[...] * conv).astype(o_ref.dtype)
    last = z_ref[tt:tt + SUBLANES, :]
    z_ref[0:SUBLANES, :] = last
    tail_ref[...] = last[SUBLANES - n_prev:, :]


def _conv(proj, w_conv, buf, batch, seq):
    _, m, d = proj.shape
    n_prev = CONV_WIDTH - 1
    tt = min(seq, CONV_ROW_TILE)
    p4 = proj.reshape(3, batch, seq, d)
    part = lambda p: pl.BlockSpec((None, None, tt, d), lambda b, i: (p, b, i, 0))
    out, tail = pl.pallas_call(
        functools.partial(_conv_kernel, tt=tt, n_prev=n_prev),
        grid=(batch, seq // tt),
        in_specs=[
            part(0), part(1), part(2),
            pl.BlockSpec((CONV_WIDTH, d), lambda b, i: (0, 0)),
            pl.BlockSpec((None, n_prev, d), lambda b, i: (b, 0, 0)),
        ],
        out_specs=[
            pl.BlockSpec((None, tt, d), lambda b, i: (b, i, 0)),
            pl.BlockSpec((None, n_prev, d), lambda b, i: (b, 0, 0)),
        ],
        out_shape=[
            jax.ShapeDtypeStruct((batch, seq, d), BF16),
            jax.ShapeDtypeStruct((batch, n_prev, d), F32),
        ],
        scratch_shapes=[pltpu.VMEM((tt + SUBLANES, d), F32)],
        compiler_params=_params("arbitrary", "arbitrary"),
        name="short_conv",
    )(p4, p4, p4, w_conv, buf)
    return out.reshape(m, d), tail


def kernel(x_prompt, x_sample, c_prompt, c_sample, cache_sb_k, cache_sb_v, page_table, state_gla, state_conv,
           norm_g, w_mod, b_mod, ffn_w_in, ffn_w_out, sb_w_in, sb_w_out, sb_bias, gla_w_in, gla_w_gate, gla_b_gate,
           gla_norm_g, gla_w_out, conv_w_in, conv_w, conv_w_out, final_norm_g):
    batch, seq, d = x_prompt.shape
    dec_b, dec_seq, _ = x_sample.shape
    depth = norm_g.shape[0]
    dh = d // SB_HEADS

    n_seq = batch + dec_b
    pad = (-n_seq) % SUBLANES
    c_all = jnp.concatenate([c_prompt, c_sample, jnp.zeros((pad, d), F32)], axis=0)
    mods = _modulation(c_all, w_mod, b_mod)
    mod_p = mods[:, :batch].reshape(depth, batch, N_MOD, 1, d)
    mod_s = jnp.repeat(mods[:, batch:n_seq], dec_seq, axis=1).reshape(depth, 1, dec_b * dec_seq, N_MOD, d)
    mod_s = mod_s.transpose(0, 1, 3, 2, 4)

    rows_s = dec_b * dec_seq
    st_p = _Stream(batch, seq, mod_p, ROW_TILE, NORM_ROW_TILE)
    st_s = _Stream(dec_b, dec_seq, mod_s, rows_s, rows_s)
    xp = x_prompt.reshape(batch * seq, d)
    xs = x_sample.reshape(rows_s, d)

    def ffn_half(xp, xs, hp, hs, l, sub, which, g_next, next_layer, next_sub):
        up, us, wo = _ffn_in(st_p, hp, hs, ffn_w_in, ffn_w_out, (l, which))
        return _out_norm(st_p, st_s, up, us, wo, xp, xs, l, sub, 0.5, g_next, next_layer, next_sub)

    n_sb = sb_w_in.shape[0]
    k_all = jnp.zeros((n_sb, batch * seq, d), F32)
    v_all = jnp.zeros((n_sb, batch * seq, d), F32)
    sbk_s, sbv_s = [], []
    gla_p, gla_s, conv_p, conv_s = [], [], [], []

    hp = _norm_mod(st_p, xp, norm_g[0, 0], 0, 0)
    hs = _norm_mod(st_s, xs, norm_g[0, 0], 0, 0)
    for l in range(depth):
        xp, xs, hp, hs = ffn_half(xp, xs, hp, hs, l, 0, 0, norm_g[l, 1], l, 1)
        kind, j = l % N_MIXERS, l // N_MIXERS
        if kind == 0:
            q_p, k_all, v_all, qkv_s, wo = _qkv_proj(st_p, hp, hs, sb_w_in, sb_w_out, j, k_all, v_all)
            op = _sb_prompt(q_p, k_all, v_all, sb_bias[j], batch, seq, j)
            os_ = _sb_sample(qkv_s, sb_bias[j], cache_sb_k, cache_sb_v, page_table, dec_b, dec_seq, j)
            sbk_s.append(qkv_s[1].reshape(dec_b, dec_seq, SB_HEADS, dh))
            sbv_s.append(qkv_s[2].reshape(dec_b, dec_seq, SB_HEADS, dh))
        elif kind == 1:
            pr_p, pr_s, wo = _in_proj(st_p, hp, hs, gla_w_in, gla_w_out, j, 3)
            g_p = _gla_gate(st_p, hp, gla_w_in, gla_w_gate, gla_b_gate, j)
            g_s = _gla_gate(st_s, hs, gla_w_in, gla_w_gate, gla_b_gate, j)
            s0 = jnp.zeros((batch,) + state_gla.shape[2:], F32)
            op, sp = _gla(pr_p, g_p, gla_norm_g[j], s0, batch, seq)
            os_, ss = _gla(pr_s, g_s, gla_norm_g[j], state_gla[j], dec_b, dec_seq)
            gla_p.append(sp)
            gla_s.append(ss)
        else:
            pr_p, pr_s, wo = _in_proj(st_p, hp, hs, conv_w_in, conv_w_out, j, 3)
            b0 = jnp.zeros((batch, CONV_WIDTH - 1, d), F32)
            op, bp = _conv(pr_p, conv_w[j], b0, batch, seq)
            os_, bs = _conv(pr_s, conv_w[j], state_conv[j], dec_b, dec_seq)
            conv_p.append(bp)
            conv_s.append(bs)
        xp, xs, hp, hs = _out_norm(st_p, st_s, op, os_, wo, xp, xs, l, 1, 1.0, norm_g[l, 2], l, 2)
        if l + 1 < depth:
            xp, xs, hp, hs = ffn_half(xp, xs, hp, hs, l, 2, 1, norm_g[l + 1, 0], l + 1, 0)
        else:
            xp, xs, hp, hs = ffn_half(xp, xs, hp, hs, l, 2, 1, final_norm_g, None, None)

    y_prompt = hp.reshape(batch, seq, d)
    y_sample = hs.reshape(dec_b, dec_seq, d)
    kv_shape = (n_sb, batch, seq, SB_HEADS, dh)
    return (y_prompt, y_sample, k_all.reshape(kv_shape), v_all.reshape(kv_shape), jnp.stack(sbk_s), jnp.stack(sbv_s),
            jnp.stack(gla_p), jnp.stack(gla_s), jnp.stack(conv_p), jnp.stack(conv_s))
```

```python
import functools

import jax
import jax.numpy as jnp
from jax import lax
from jax.experimental import pallas as pl
from jax.experimental.pallas import tpu as pltpu

F32 = jnp.float32
BF16 = jnp.bfloat16

N_MIXERS = 3
N_MOD = 9
SB_HEADS = 16
GLA_HEADS = 4
GLA_RANK = 16
GLA_TAU = 16.0
GLA_CHUNK = 64
CONV_WIDTH = 3
RMS_EPS = 1e-6
LOG2E = 1.4426950408889634

LANES = 128
SUBLANES = 8
VMEM_BYTES_V7X = 64 * 1024 * 1024
VMEM_LIMIT = VMEM_BYTES_V7X - 8 * 1024 * 1024

ROW_TILE = 1024
COL_TILE = 1024
QKV_COL_TILE = 512
FFN_COL_TILE = 512
CAST_ROWS = 512
OUT_ROW_TILE_DEEP = 256
OUT_ROW_TILE = 512
NORM_ROW_TILE = 1024
CONV_ROW_TILE = 512
MOD_COL_TILE = 1024
SB_K_TILE = 256
PAGES_PER_STEP = 8


def _params(*sem):
    return pltpu.CompilerParams(dimension_semantics=sem, vmem_limit_bytes=VMEM_LIMIT)


def _sigmoid(x):
    return 1.0 / (1.0 + jnp.exp(-x))


def _softplus(z):
    return jnp.maximum(z, 0.0) + jnp.log1p(jnp.exp(-jnp.abs(z)))


def _split_bf16(x):
    hi = x.astype(BF16)
    lo = (x - hi.astype(F32)).astype(BF16)
    return hi, lo


def _dot(a, b):
    return jnp.dot(a, b, preferred_element_type=F32)


def _dot_nt(a, b):
    return lax.dot_general(a, b, (((1,), (1,)), ((), ())), preferred_element_type=F32)


def _dot_tn(a, b):
    return lax.dot_general(a, b, (((0,), (0,)), ((), ())), preferred_element_type=F32)


def _mod_kernel(c_ref, w_ref, b_ref, o_ref):
    c = c_ref[...]
    s = (c * _sigmoid(c)).astype(BF16)
    o_ref[...] = _dot(s, w_ref[...].astype(BF16)) + b_ref[...]


def _modulation(c_all, w_mod, b_mod):
    depth, d, n = w_mod.shape
    rows = c_all.shape[0]
    tn = MOD_COL_TILE
    return pl.pallas_call(
        _mod_kernel,
        grid=(depth, n // tn),
        in_specs=[
            pl.BlockSpec((rows, d), lambda l, j: (0, 0)),
            pl.BlockSpec((None, d, tn), lambda l, j: (l, 0, j)),
            pl.BlockSpec((None, 1, tn), lambda l, j: (l, 0, j)),
        ],
        out_specs=pl.BlockSpec((None, rows, tn), lambda l, j: (l, 0, j)),
        out_shape=jax.ShapeDtypeStruct((depth, rows, n), F32),
        compiler_params=_params("arbitrary", "arbitrary"),
        name="modulation",
    )(c_all, w_mod, b_mod.reshape(depth, 1, n))


class _Stream:
    def __init__(self, batch, seq, mod, row_tile, norm_row_tile):
        self.batch, self.seq, self.mod = batch, seq, mod
        self.rows = batch * seq
        self.row_tile, self.norm_row_tile = row_tile, norm_row_tile
        self.per_tile_mod = mod.shape[3] != 1

    def mod_spec(self, layer, chunk, tm, width, col_of, row_of):
        r = self.mod.shape[3]
        if self.per_tile_mod:
            assert r == tm
            group = lambda *g: 0
        else:
            tiles_per_seq = self.seq // tm
            group = lambda *g: row_of(*g) // tiles_per_seq
        return pl.BlockSpec((None, None, None, r, width),
                            lambda *g: (layer, group(*g), chunk, 0, col_of(*g)))


def _norm_mod_kernel(x_ref, g_ref, shift_ref, scale_ref, o_ref):
    x = x_ref[...]
    y = x * lax.rsqrt(jnp.mean(x * x, axis=-1, keepdims=True) + RMS_EPS) * g_ref[...]
    o_ref[...] = (y * (1.0 + scale_ref[...]) + shift_ref[...]).astype(o_ref.dtype)


def _norm_mod(st, x, g, layer, sub):
    m, d = x.shape
    tm = st.norm_row_tile
    row_of = lambda i: i
    col_of = lambda i: 0
    return pl.pallas_call(
        _norm_mod_kernel,
        grid=(m // tm,),
        in_specs=[
            pl.BlockSpec((tm, d), lambda i: (i, 0)),
            pl.BlockSpec((1, d), lambda i: (0, 0)),
            st.mod_spec(layer, 3 * sub, tm, d, col_of, row_of),
            st.mod_spec(layer, 3 * sub + 1, tm, d, col_of, row_of),
        ],
        out_specs=pl.BlockSpec((tm, d), lambda i: (i, 0)),
        out_shape=jax.ShapeDtypeStruct((m, d), BF16),
        compiler_params=_params("arbitrary"),
        name="norm_mod",
    )(x, g.reshape(1, d), st.mod, st.mod)


def _cast_chunk_specs(w_out, layer_j, n_steps):
    k_out, d = w_out.shape[-2:]
    n_chunks = k_out // CAST_ROWS
    assert k_out % CAST_ROWS == 0 and n_chunks <= n_steps
    chunk = lambda j: jnp.minimum(j, n_chunks - 1)
    in_spec = pl.BlockSpec((None, CAST_ROWS, d), lambda j, i: (layer_j, chunk(j), 0))
    out_spec = pl.BlockSpec((CAST_ROWS, d), lambda j, i: (chunk(j), 0))
    return in_spec, out_spec, jax.ShapeDtypeStruct((k_out, d), BF16), n_chunks


def _cast_chunk(wo_ref, wob_ref, n_chunks):
    @pl.when((pl.program_id(1) == 0) & (pl.program_id(0) < n_chunks))
    def _():
        wob_ref[...] = wo_ref[...].astype(BF16)


def _mm_kernel(a_ref, as_ref, w_ref, wo_ref, o_ref, os_ref, wob_ref, wb_ref, *, n_chunks):
    @pl.when(pl.program_id(1) == 0)
    def _():
        wb_ref[...] = w_ref[...].astype(BF16)
        os_ref[...] = _dot(as_ref[...], wb_ref[...])

    _cast_chunk(wo_ref, wob_ref, n_chunks)
    o_ref[...] = _dot(a_ref[...], wb_ref[...])


def _in_proj(st, a, a_s, w, w_out, layer_j, n_parts):
    m, k = a.shape
    ms = a_s.shape[0]
    tm, tn = st.row_tile, COL_TILE
    tpp = k // tn
    wo_in, wo_out, wo_shape, n_chunks = _cast_chunk_specs(w_out, layer_j, n_parts * tpp)
    return pl.pallas_call(
        functools.partial(_mm_kernel, n_chunks=n_chunks),
        grid=(n_parts * tpp, m // tm),
        in_specs=[
            pl.BlockSpec((tm, k), lambda j, i: (i, 0)),
            pl.BlockSpec((ms, k), lambda j, i: (0, 0)),
            pl.BlockSpec((None, k, tn), lambda j, i: (layer_j, 0, j)),
            wo_in,
        ],
        out_specs=[
            pl.BlockSpec((None, tm, tn), lambda j, i: (j // tpp, i, j % tpp)),
            pl.BlockSpec((None, ms, tn), lambda j, i: (j // tpp, 0, j % tpp)),
            wo_out,
        ],
        out_shape=[jax.ShapeDtypeStruct((n_parts, m, k), F32), jax.ShapeDtypeStruct((n_parts, ms, k), F32), wo_shape],
        scratch_shapes=[pltpu.VMEM((k, tn), BF16)],
        compiler_params=_params("arbitrary", "arbitrary"),
        name="in_proj",
    )(a, a_s, w, w_out)


def _qkv_kernel(a_ref, as_ref, w_ref, wo_ref, *refs, tpp, n_chunks, layer_j, creates):
    q_ref, k_ref, v_ref, os_ref, wob_ref, wb_ref = refs[0 if creates else 2:]
    part = pl.program_id(0) // tpp

    @pl.when(pl.program_id(1) == 0)
    def _():
        wb_ref[...] = w_ref[...].astype(BF16)
        os_ref[...] = _dot(as_ref[...], wb_ref[...])

    _cast_chunk(wo_ref, wob_ref, n_chunks)
    y = _dot(a_ref[...], wb_ref[...])

    def store_stacked(ref):
        if creates:
            for l in range(ref.shape[0]):
                ref[l] = y if l == layer_j else jnp.zeros_like(y)
        else:
            ref[...] = y

    @pl.when(part == 0)
    def _():
        q_ref[...] = y

    @pl.when(part == 1)
    def _():
        store_stacked(k_ref)

    @pl.when(part == 2)
    def _():
        store_stacked(v_ref)


def _qkv_proj(st, a, a_s, w, w_out, layer_j, n_layers, k_all, v_all):
    m, k = a.shape
    ms = a_s.shape[0]
    tm, tn = st.row_tile, QKV_COL_TILE
    tpp = k // tn
    ni = m // tm
    creates = k_all is None
    wo_in, wo_out, wo_shape, n_chunks = _cast_chunk_specs(w_out, layer_j, 3 * tpp)

    def rows(j, i, part):
        return jnp.where(j < part * tpp, 0, jnp.where(j < (part + 1) * tpp, i, ni - 1))

    def cols(j, part):
        return jnp.clip(j - part * tpp, 0, tpp - 1)

    def stacked_spec(part):
        if creates:
            return pl.BlockSpec((n_layers, tm, tn), lambda j, i: (0, rows(j, i, part), cols(j, part)))
        return pl.BlockSpec((None, tm, tn), lambda j, i: (layer_j, rows(j, i, part), cols(j, part)))

    stacked = jax.ShapeDtypeStruct((n_layers, m, k), F32)
    any_spec = pl.BlockSpec(memory_space=pl.ANY)
    return pl.pallas_call(
        functools.partial(_qkv_kernel, tpp=tpp, n_chunks=n_chunks, layer_j=layer_j, creates=creates),
        grid=(3 * tpp, ni),
        in_specs=[
            pl.BlockSpec((tm, k), lambda j, i: (i, 0)),
            pl.BlockSpec((ms, k), lambda j, i: (0, 0)),
            pl.BlockSpec((None, k, tn), lambda j, i: (layer_j, 0, j)),
            wo_in,
        ] + ([] if creates else [any_spec, any_spec]),
        out_specs=[
            pl.BlockSpec((tm, tn), lambda j, i: (rows(j, i, 0), cols(j, 0))),
            stacked_spec(1),
            stacked_spec(2),
            pl.BlockSpec((None, ms, tn), lambda j, i: (j // tpp, 0, j % tpp)),
            wo_out,
        ],
        out_shape=[jax.ShapeDtypeStruct((m, k), F32), stacked, stacked, jax.ShapeDtypeStruct((3, ms, k), F32),
                   wo_shape],
        scratch_shapes=[pltpu.VMEM((k, tn), BF16)],
        input_output_aliases={} if creates else {4: 1, 5: 2},
        compiler_params=_params("arbitrary", "arbitrary"),
        name="qkv_proj",
    )(a, a_s, w, w_out, *(() if creates else (k_all, v_all)))


def _swiglu(h, wa, wb):
    a = _dot(h, wa)
    return (a * _sigmoid(a)) * _dot(h, wb)


def _ffn_in_kernel(a_ref, as_ref, wa_ref, wb_ref, wo_ref, o_ref, os_ref, wob_ref, wab_ref, wbb_ref):
    @pl.when(pl.program_id(1) == 0)
    def _():
        wab_ref[...] = wa_ref[...].astype(BF16)
        wbb_ref[...] = wb_ref[...].astype(BF16)
        os_ref[...] = _swiglu(as_ref[...], wab_ref[...], wbb_ref[...]).astype(os_ref.dtype)

    wob_ref[...] = wo_ref[...].astype(BF16)
    o_ref[...] = _swiglu(a_ref[...], wab_ref[...], wbb_ref[...]).astype(o_ref.dtype)


def _ffn_in(st, a, a_s, w, w_out, widx):
    m, k = a.shape
    ms = a_s.shape[0]
    hidden, d = w_out.shape[-2:]
    tm, tn = st.row_tile, FFN_COL_TILE
    nj, ni = hidden // tn, m // tm
    cast_rows = hidden // (nj * ni)
    assert cast_rows * nj * ni == hidden and cast_rows % 16 == 0
    wspec = lambda off: pl.BlockSpec((None, None, k, tn), lambda j, i: widx + (0, j + off))
    return pl.pallas_call(
        _ffn_in_kernel,
        grid=(nj, ni),
        in_specs=[pl.BlockSpec((tm, k), lambda j, i: (i, 0)), pl.BlockSpec((ms, k), lambda j, i: (0, 0)),
                  wspec(0), wspec(nj),
                  pl.BlockSpec((None, None, cast_rows, d), lambda j, i: widx + (j * ni + i, 0))],
        out_specs=[pl.BlockSpec((tm, tn), lambda j, i: (i, j)), pl.BlockSpec((ms, tn), lambda j, i: (0, j)),
                   pl.BlockSpec((cast_rows, d), lambda j, i: (j * ni + i, 0))],
        out_shape=[jax.ShapeDtypeStruct((m, hidden), BF16), jax.ShapeDtypeStruct((ms, hidden), BF16),
                   jax.ShapeDtypeStruct((hidden, d), BF16)],
        scratch_shapes=[pltpu.VMEM((k, tn), BF16), pltpu.VMEM((k, tn), BF16)],
        compiler_params=_params("arbitrary", "arbitrary"),
        name="ffn_in",
    )(a, a_s, w, w, w_out)


def _out_norm_kernel(u_ref, us_ref, w_ref, x_ref, xs_ref, gate_ref, gates_ref, g_ref, shift_ref, shifts_ref,
                     scale_ref, scales_ref, o_ref, os_ref, h_ref, hs_ref, *, coef, final):
    def tile(u_ref, x_ref, gate_ref, shift_ref, scale_ref, o_ref, h_ref):
        gate = gate_ref[...]
        if coef != 1.0:
            gate = coef * gate
        x = x_ref[...] + gate * _dot(u_ref[...], w_ref[...])
        o_ref[...] = x
        y = x * lax.rsqrt(jnp.mean(x * x, axis=-1, keepdims=True) + RMS_EPS) * g_ref[...]
        if not final:
            y = y * (1.0 + scale_ref[...]) + shift_ref[...]
        h_ref[...] = y.astype(h_ref.dtype)

    @pl.when(pl.program_id(0) == 0)
    def _():
        tile(us_ref, xs_ref, gates_ref, shifts_ref, scales_ref, os_ref, hs_ref)

    tile(u_ref, x_ref, gate_ref, shift_ref, scale_ref, o_ref, h_ref)


def _out_norm(st, st_s, u, u_s, w_bf16, x, x_s, layer, sub, coef, g_next, next_layer, next_sub):
    m, k = u.shape
    ms = u_s.shape[0]
    d = x.shape[1]
    final = next_layer is None
    tm = OUT_ROW_TILE if k <= d else OUT_ROW_TILE_DEEP
    row_of, col_of, zero = (lambda i: i), (lambda i: 0), (lambda i: 0)
    nl, ns = (layer, sub) if final else (next_layer, next_sub)

    def mods(chunk_layer, chunk):
        return [st.mod_spec(chunk_layer, chunk, tm, d, col_of, row_of),
                st_s.mod_spec(chunk_layer, chunk, ms, d, col_of, zero)]

    h_dtype = F32 if final else BF16
    return pl.pallas_call(
        functools.partial(_out_norm_kernel, coef=coef, final=final),
        grid=(m // tm,),
        in_specs=[
            pl.BlockSpec((tm, k), lambda i: (i, 0)),
            pl.BlockSpec((ms, k), lambda i: (0, 0)),
            pl.BlockSpec((k, d), lambda i: (0, 0), pipeline_mode=pl.Buffered(1)),
            pl.BlockSpec((tm, d), lambda i: (i, 0)),
            pl.BlockSpec((ms, d), lambda i: (0, 0)),
        ] + mods(layer, 3 * sub + 2) + [pl.BlockSpec((1, d), lambda i: (0, 0))]
          + mods(nl, 3 * ns) + mods(nl, 3 * ns + 1),
        out_specs=[pl.BlockSpec((tm, d), lambda i: (i, 0)), pl.BlockSpec((ms, d), lambda i: (0, 0)),
                   pl.BlockSpec((tm, d), lambda i: (i, 0)), pl.BlockSpec((ms, d), lambda i: (0, 0))],
        out_shape=[jax.ShapeDtypeStruct((m, d), F32), jax.ShapeDtypeStruct((ms, d), F32),
                   jax.ShapeDtypeStruct((m, d), h_dtype), jax.ShapeDtypeStruct((ms, d), h_dtype)],
        compiler_params=_params("arbitrary"),
        name="out_norm",
    )(u, u_s, w_bf16, x, x_s, st.mod, st_s.mod, g_next.reshape(1, d), st.mod, st_s.mod, st.mod, st_s.mod)


def _softplus2(z2):
    neg_abs = pltpu.bitcast(pltpu.bitcast(z2, jnp.uint32) | jnp.uint32(0x80000000), F32)
    return jnp.maximum(z2, 0.0) + jnp.log(1.0 + jnp.exp2(neg_abs)) * LOG2E


def _sb_block(q, kb, vb, carry, mask, *, tri, bias2, scale2):
    z = _dot_nt(q, kb) * scale2 + bias2
    sp = _softplus2(z)
    log_not = -sp
    if mask is not None:
        log_not = jnp.where(mask, log_not, 0.0)
    between = _dot(log_not.astype(BF16), tri) + carry
    a = jnp.exp2((z - sp) + between)
    if mask is not None:
        a = jnp.where(mask, a, 0.0)
    new_carry = between[:, 0:1] + log_not[:, 0:1]
    return _dot(a.astype(BF16), vb), new_carry


def _sb_prompt_kernel(bias_ref, q_ref, k_ref, v_ref, o_ref, kb_ref, vb_ref, qb_ref, acc_ref, carry_ref,
                      *, tk, scale2):
    kb_ref[...] = k_ref[...].astype(BF16)
    vb_ref[...] = v_ref[...].astype(BF16)
    qb_ref[...] = q_ref[...].astype(BF16)
    tq = 2 * tk
    row = lax.broadcasted_iota(jnp.int32, (tk, tk), 0)
    col = lax.broadcasted_iota(jnp.int32, (tk, tk), 1)
    tri = jnp.where(row > col, 1.0, 0.0).astype(BF16)
    causal = col < row
    block = functools.partial(_sb_block, tri=tri, bias2=bias_ref[pl.program_id(1)] * LOG2E, scale2=scale2)

    def keys(j):
        start = pl.multiple_of(j * tk, tk)
        return kb_ref[pl.ds(start, tk), :], vb_ref[pl.ds(start, tk), :]

    zero = jnp.zeros((tk, 1), F32)

    def query_tile(i, _):
        q_lo = qb_ref[pl.ds(pl.multiple_of(i * tq, tq), tk), :]
        q_hi = qb_ref[pl.ds(pl.multiple_of(i * tq + tk, tk), tk), :]
        kb, vb = keys(2 * i + 1)
        out_b, carry_b = block(q_hi, kb, vb, zero, causal)
        kb, vb = keys(2 * i)
        out_a, carry_a = block(q_lo, kb, vb, zero, causal)
        out_c, carry_c = block(q_hi, kb, vb, carry_b, None)
        acc_ref[:tk, :] = out_a
        acc_ref[tk:, :] = out_b + out_c
        carry_ref[:tk, :] = carry_a
        carry_ref[tk:, :] = carry_c

        def body(t, _):
            p = i - 1 - t
            q = qb_ref[pl.ds(pl.multiple_of(i * tq, tq), tq), :]
            kb, vb = keys(2 * p + 1)
            out1, carry = block(q, kb, vb, carry_ref[...], None)
            kb, vb = keys(2 * p)
            out2, carry = block(q, kb, vb, carry, None)
            acc_ref[...] += out1 + out2
            carry_ref[...] = carry
            return 0

        lax.fori_loop(0, i, body, 0)
        o_ref[pl.ds(pl.multiple_of(i * tq, tq), tq), :] = acc_ref[...].astype(o_ref.dtype)
        return 0

    lax.fori_loop(0, q_ref.shape[0] // tq, query_tile, 0)


def _sb_prompt(q, k_all, v_all, bias, batch, seq, layer_j):
    m, width = q.shape
    dh = width // SB_HEADS
    tk = SB_K_TILE
    tq = 2 * tk
    assert seq % tq == 0
    kv_spec = pl.BlockSpec((None, None, seq, dh), lambda b, h: (layer_j, b, 0, h))
    stacked = (k_all.shape[0], batch, seq, width)
    out = pl.pallas_call(
        functools.partial(_sb_prompt_kernel, tk=tk, scale2=dh ** -0.5 * LOG2E),
        grid=(batch, SB_HEADS),
        in_specs=[
            pl.BlockSpec(memory_space=pltpu.SMEM),
            pl.BlockSpec((None, seq, dh), lambda b, h: (b, 0, h)),
            kv_spec,
            kv_spec,
        ],
        out_specs=pl.BlockSpec((None, seq, dh), lambda b, h: (b, 0, h)),
        out_shape=jax.ShapeDtypeStruct((batch, seq, width), BF16),
        scratch_shapes=[
            pltpu.VMEM((seq, dh), BF16), pltpu.VMEM((seq, dh), BF16), pltpu.VMEM((seq, dh), BF16),
            pltpu.VMEM((tq, dh), F32), pltpu.VMEM((tq, 1), F32),
        ],
        compiler_params=_params("arbitrary", "arbitrary"),
        name="sb_prompt",
    )(bias, q.reshape(batch, seq, width), k_all.reshape(stacked), v_all.reshape(stacked))
    return out.reshape(m, width)


def _sb_sample_page(q, bias2, k_ref, v_ref, carry, valid, tri, *, scale2):
    n_pairs = SUBLANES
    n_lanes = tri.shape[0]
    rows = q.shape[0] // n_pairs
    z = jnp.concatenate(
        [_dot_nt(q[p * rows:(p + 1) * rows], k_ref[pl.ds(p, n_lanes, stride=n_pairs), :].astype(BF16))
         for p in range(n_pairs)], axis=0) * scale2 + bias2
    sp = _softplus2(z)
    log_not = jnp.where(valid, -sp, 0.0)
    between = _dot(log_not.astype(BF16), tri) + carry
    a = jnp.where(valid, jnp.exp2((z - sp) + between), 0.0).astype(BF16)
    out = jnp.concatenate(
        [_dot(a[p * rows:(p + 1) * rows], v_ref[pl.ds(p, n_lanes, stride=n_pairs), :].astype(BF16))
         for p in range(n_pairs)], axis=0)
    return out, carry + jnp.sum(log_not, axis=1, keepdims=True)


def _sb_sample_kernel(pt_ref, q_ref, bias_ref, knew_ref, vnew_ref, *refs, n_pages_step, n_new, scale2):
    k_refs = refs[:n_pages_step]
    v_refs = refs[n_pages_step:2 * n_pages_step]
    o_ref, acc_ref, carry_ref = refs[2 * n_pages_step:]
    s = pl.program_id(1)
    q = q_ref[...]
    bias2 = bias_ref[...] * LOG2E
    n_rows = q.shape[0]

    def masks(n_lanes):
        r = lax.broadcasted_iota(jnp.int32, (n_rows, n_lanes), 0)
        c = lax.broadcasted_iota(jnp.int32, (n_rows, n_lanes), 1)
        same_head = (r // n_new) % 2 == c % 2
        tr = lax.broadcasted_iota(jnp.int32, (n_lanes, n_lanes), 0)
        tc = lax.broadcasted_iota(jnp.int32, (n_lanes, n_lanes), 1)
        tri = jnp.where((tr // 2 > tc // 2) & (tr % 2 == tc % 2), 1.0, 0.0).astype(BF16)
        return r, c, same_head, tri

    @pl.when(s == 0)
    def _():
        r, c, same_head, tri = masks(knew_ref.shape[0] // SUBLANES)
        valid = same_head & (c // 2 < r % n_new)
        out, carry = _sb_sample_page(q, bias2, knew_ref, vnew_ref, jnp.zeros((n_rows, 1), F32), valid, tri,
                                     scale2=scale2)
        acc_ref[...] = out
        carry_ref[...] = carry

    _, _, same_head, tri = masks(k_refs[0].shape[0] // SUBLANES)
    carry = carry_ref[...]
    acc = acc_ref[...]
    for g in range(n_pages_step):
        out, carry = _sb_sample_page(q, bias2, k_refs[g], v_refs[g], carry, same_head, tri, scale2=scale2)
        acc = acc + out
    acc_ref[...] = acc
    carry_ref[...] = carry

    @pl.when(s == pl.num_programs(1) - 1)
    def _():
        dh = q.shape[1]
        for p in range(SUBLANES):
            for e in range(2):
                h = e * SUBLANES + p
                src = (p * 2 + e) * n_new
                o_ref[:, h * dh:(h + 1) * dh] = acc_ref[src:src + n_new, :].astype(o_ref.dtype)


def _sb_sample(qkv, bias, cache_k, cache_v, page_table, batch, n_new, layer_j):
    _, m, width = qkv.shape
    dh = width // SB_HEADS
    n_layers, n_pool, page = cache_k.shape[:3]
    n_pages = page_table.shape[1]
    g = PAGES_PER_STEP
    assert n_pages % g == 0 and SB_HEADS == 2 * SUBLANES and dh == LANES
    n_rows = SB_HEADS * n_new
    pad_tok = LANES // 2 - n_new
    q = qkv[0].reshape(batch, n_new, 2, SUBLANES, dh).transpose(0, 3, 2, 1, 4).reshape(batch, n_rows, dh)
    bias_rows = jnp.repeat(bias.reshape(2, SUBLANES).T.reshape(-1), n_new).reshape(n_rows, 1)

    def new_rows(part):
        x = qkv[part].reshape(batch, n_new, SB_HEADS, dh)
        return jnp.pad(x, ((0, 0), (0, pad_tok), (0, 0), (0, 0))).reshape(batch, (n_new + pad_tok) * SB_HEADS, dh)

    ck = cache_k.reshape(n_layers, n_pool, page * SB_HEADS, dh)
    cv = cache_v.reshape(n_layers, n_pool, page * SB_HEADS, dh)

    def page_spec(gi):
        return pl.BlockSpec((None, None, page * SB_HEADS, dh),
                            lambda b, s, pt: (layer_j, pt[b, n_pages - 1 - (s * g + gi)], 0, 0))

    new_spec = pl.BlockSpec((None, (n_new + pad_tok) * SB_HEADS, dh), lambda b, s, pt: (b, 0, 0))
    grid_spec = pltpu.PrefetchScalarGridSpec(
        num_scalar_prefetch=1,
        grid=(batch, n_pages // g),
        in_specs=[
            pl.BlockSpec((None, n_rows, dh), lambda b, s, pt: (b, 0, 0)),
            pl.BlockSpec((n_rows, 1), lambda b, s, pt: (0, 0)),
            new_spec,
            new_spec,
        ] + [page_spec(gi) for gi in range(g)] * 2,
        out_specs=pl.BlockSpec((None, n_new, width), lambda b, s, pt: (b, 0, 0)),
        scratch_shapes=[pltpu.VMEM((n_rows, dh), F32), pltpu.VMEM((n_rows, 1), F32)],
    )
    out = pl.pallas_call(
        functools.partial(_sb_sample_kernel, n_pages_step=g, n_new=n_new, scale2=dh ** -0.5 * LOG2E),
        grid_spec=grid_spec,
        out_shape=jax.ShapeDtypeStruct((batch, n_new, width), BF16),
        compiler_params=_params("arbitrary", "arbitrary"),
        name="sb_sample",
    )(page_table, q.astype(BF16), bias_rows, new_rows(1), new_rows(2), *([ck] * g), *([cv] * g))
    return out.reshape(m, width)


def _gla_gate_kernel(h_ref, wr_ref, wg_ref, b_ref, o_ref):
    glr = _dot(h_ref[...], wr_ref[...].astype(BF16))
    x = _dot(glr.astype(BF16), wg_ref[...].astype(BF16)) + b_ref[...]
    o_ref[...] = -_softplus(-x) / GLA_TAU


def _gla_gate(st, h, w_in, w_gate, b_gate, j):
    m, d = h.shape
    dk = w_gate.shape[-1]
    rank = w_gate.shape[1]
    w_r = jnp.pad(w_in[j][:, w_in.shape[-1] - rank:], ((0, 0), (0, LANES - rank)))
    w_g = jnp.pad(w_gate[j], ((0, LANES - rank), (0, 0)))
    tm = st.norm_row_tile
    return pl.pallas_call(
        _gla_gate_kernel,
        grid=(m // tm,),
        in_specs=[
            pl.BlockSpec((tm, d), lambda i: (i, 0)),
            pl.BlockSpec((d, LANES), lambda i: (0, 0)),
            pl.BlockSpec((LANES, dk), lambda i: (0, 0)),
            pl.BlockSpec((1, dk), lambda i: (0, 0)),
        ],
        out_specs=pl.BlockSpec((tm, dk), lambda i: (i, 0)),
        out_shape=jax.ShapeDtypeStruct((m, dk), F32),
        compiler_params=_params("arbitrary"),
        name="gla_gate",
    )(h, w_r, w_g, b_gate[j].reshape(1, dk))


def _gla_kernel(q_ref, k_ref, v_ref, r_ref, g_ref, ng_ref, s0_ref, o_ref, s_ref, *, chunk, n_chunks, scale):
    s_ref[...] = s0_ref[...]
    row = lax.broadcasted_iota(jnp.int32, (chunk, chunk), 0)
    col = lax.broadcasted_iota(jnp.int32, (chunk, chunk), 1)
    keep = col <= row
    lower = jnp.where(keep, 1.0, 0.0).astype(BF16)
    ones = jnp.ones((chunk, LANES), BF16)
    ng = ng_ref[...]
    dv = v_ref.shape[-1]

    def body(c, _):
        rows = slice(None) if n_chunks == 1 else pl.ds(pl.multiple_of(c * chunk, chunk), chunk)
        g = g_ref[rows, :]
        g_hi, g_lo = _split_bf16(g)
        bcum = _dot(lower, g_hi) + _dot(lower, g_lo)
        b_last = bcum[chunk - 1:chunk, :]
        q = q_ref[rows, :] * scale
        k = k_ref[rows, :]
        v = v_ref[rows, :].astype(BF16)
        qe = (q * jnp.exp(bcum)).astype(BF16)
        ke = (k * jnp.exp(-bcum)).astype(BF16)
        kl = (k * jnp.exp(b_last - bcum)).astype(BF16)
        state = s_ref[...]
        scores = jnp.where(keep, _dot_nt(qe, ke), 0.0)
        o = _dot(qe, state.astype(BF16)) + _dot(scores.astype(BF16), v)
        total = _dot_tn(g_hi, ones) + _dot_tn(g_lo, ones)
        decay = jnp.exp(total)
        s_ref[...] = jnp.concatenate([decay] * (dv // LANES), axis=1) * state + _dot_tn(kl, v)
        on = o * lax.rsqrt(jnp.mean(o * o, axis=-1, keepdims=True) + RMS_EPS) * ng
        r = r_ref[rows, :]
        o_ref[rows, :] = (on * (r * _sigmoid(r))).astype(o_ref.dtype)
        return 0

    if n_chunks == 1:
        body(0, 0)
    else:
        unroll = 2 if n_chunks % 2 == 0 else 1

        def body_unrolled(c, _):
            for u in range(unroll):
                body(c * unroll + u, 0)
            return 0

        lax.fori_loop(0, n_chunks // unroll, body_unrolled, 0)


def _gla(proj, g, norm_g, s0, batch, seq):
    _, m, d = proj.shape
    dk = g.shape[1]
    dkh, dvh = dk // GLA_HEADS, d // GLA_HEADS
    chunk = min(GLA_CHUNK, seq)
    p4 = proj.reshape(3, batch, seq, d)
    g3 = g.reshape(batch, seq, dk)
    nkh = dk // dkh
    out, state = pl.pallas_call(
        functools.partial(_gla_kernel, chunk=chunk, n_chunks=seq // chunk, scale=dkh ** -0.5),
        grid=(batch, GLA_HEADS),
        in_specs=[
            pl.BlockSpec((None, None, seq, dkh), lambda b, h: (0, b, 0, h)),
            pl.BlockSpec((None, None, seq, dkh), lambda b, h: (0, b, 0, nkh + h)),
            pl.BlockSpec((None, None, seq, dvh), lambda b, h: (1, b, 0, h)),
            pl.BlockSpec((None, None, seq, dvh), lambda b, h: (2, b, 0, h)),
            pl.BlockSpec((None, seq, dkh), lambda b, h: (b, 0, h)),
            pl.BlockSpec((1, dvh), lambda b, h: (0, 0)),
            pl.BlockSpec((None, None, dkh, dvh), lambda b, h: (b, h, 0, 0)),
        ],
        out_specs=[
            pl.BlockSpec((None, seq, dvh), lambda b, h: (b, 0, h)),
            pl.BlockSpec((None, None, dkh, dvh), lambda b, h: (b, h, 0, 0)),
        ],
        out_shape=[
            jax.ShapeDtypeStruct((batch, seq, d), BF16),
            jax.ShapeDtypeStruct((batch, GLA_HEADS, dkh, dvh), F32),
        ],
        compiler_params=_params("arbitrary", "arbitrary"),
        name="gla",
    )(p4, p4, p4, p4, g3, norm_g.reshape(1, dvh), s0)
    return out.reshape(m, d), state


def _conv_kernel(gb_ref, gc_ref, u_ref, w_ref, buf_ref, o_ref, tail_ref, z_ref, *, tt, n_prev):
    i = pl.program_id(1)

    @pl.when(i == 0)
    def _():
        z_ref[SUBLANES - n_prev:SUBLANES, :] = buf_ref[...]

    z = gc_ref[...] * u_ref[...]
    z_ref[SUBLANES:SUBLANES + tt, :] = z
    conv = w_ref[n_prev:n_prev + 1, :] * z
    for j in range(n_prev):
        conv = conv + w_ref[j:j + 1, :] * z_ref[SUBLANES - n_prev + j:SUBLANES - n_prev + j + tt, :]
    o_ref[...] = (gb_ref[...] * conv).astype(o_ref.dtype)
    last = z_ref[tt:tt + SUBLANES, :]
    z_ref[0:SUBLANES, :] = last
    tail_ref[...] = last[SUBLANES - n_prev:, :]


def _conv(proj, w_conv, buf, batch, seq):
    _, m, d = proj.shape
    n_prev = CONV_WIDTH - 1
    tt = min(seq, CONV_ROW_TILE)
    p4 = proj.reshape(3, batch, seq, d)
    part = lambda p: pl.BlockSpec((None, None, tt, d), lambda b, i: (p, b, i, 0))
    out, tail = pl.pallas_call(
        functools.partial(_conv_kernel, tt=tt, n_prev=n_prev),
        grid=(batch, seq // tt),
        in_specs=[
            part(0), part(1), part(2),
            pl.BlockSpec((CONV_WIDTH, d), lambda b, i: (0, 0)),
            pl.BlockSpec((None, n_prev, d), lambda b, i: (b, 0, 0)),
        ],
        out_specs=[
            pl.BlockSpec((None, tt, d), lambda b, i: (b, i, 0)),
            pl.BlockSpec((None, n_prev, d), lambda b, i: (b, 0, 0)),
        ],
        out_shape=[
            jax.ShapeDtypeStruct((batch, seq, d), BF16),
            jax.ShapeDtypeStruct((batch, n_prev, d), F32),
        ],
        scratch_shapes=[pltpu.VMEM((tt + SUBLANES, d), F32)],
        compiler_params=_params("arbitrary", "arbitrary"),
        name="short_conv",
    )(p4, p4, p4, w_conv, buf)
    return out.reshape(m, d), tail


def kernel(x_prompt, x_sample, c_prompt, c_sample, cache_sb_k, cache_sb_v, page_table, state_gla, state_conv,
           norm_g, w_mod, b_mod, ffn_w_in, ffn_w_out, sb_w_in, sb_w_out, sb_bias, gla_w_in, gla_w_gate, gla_b_gate,
           gla_norm_g, gla_w_out, conv_w_in, conv_w, conv_w_out, final_norm_g):
    batch, seq, d = x_prompt.shape
    dec_b, dec_seq, _ = x_sample.shape
    depth = norm_g.shape[0]
    dh = d // SB_HEADS

    n_seq = batch + dec_b
    pad = (-n_seq) % SUBLANES
    c_all = jnp.concatenate([c_prompt, c_sample, jnp.zeros((pad, d), F32)], axis=0)
    mods = _modulation(c_all, w_mod, b_mod)
    mod_p = mods[:, :batch].reshape(depth, batch, N_MOD, 1, d)
    mod_s = jnp.repeat(mods[:, batch:n_seq], dec_seq, axis=1).reshape(depth, 1, dec_b * dec_seq, N_MOD, d)
    mod_s = mod_s.transpose(0, 1, 3, 2, 4)

    rows_s = dec_b * dec_seq
    st_p = _Stream(batch, seq, mod_p, ROW_TILE, NORM_ROW_TILE)
    st_s = _Stream(dec_b, dec_seq, mod_s, rows_s, rows_s)
    xp = x_prompt.reshape(batch * seq, d)
    xs = x_sample.reshape(rows_s, d)

    def ffn_half(xp, xs, hp, hs, l, sub, which, g_next, next_layer, next_sub):
        up, us, wo = _ffn_in(st_p, hp, hs, ffn_w_in, ffn_w_out, (l, which))
        return _out_norm(st_p, st_s, up, us, wo, xp, xs, l, sub, 0.5, g_next, next_layer, next_sub)

    n_sb = sb_w_in.shape[0]
    k_all = v_all = None
    sbk_s, sbv_s = [], []
    gla_p, gla_s, conv_p, conv_s = [], [], [], []

    hp = _norm_mod(st_p, xp, norm_g[0, 0], 0, 0)
    hs = _norm_mod(st_s, xs, norm_g[0, 0], 0, 0)
    for l in range(depth):
        xp, xs, hp, hs = ffn_half(xp, xs, hp, hs, l, 0, 0, norm_g[l, 1], l, 1)
        kind, j = l % N_MIXERS, l // N_MIXERS
        if kind == 0:
            q_p, k_all, v_all, qkv_s, wo = _qkv_proj(st_p, hp, hs, sb_w_in, sb_w_out, j, n_sb, k_all, v_all)
            op = _sb_prompt(q_p, k_all, v_all, sb_bias[j], batch, seq, j)
            os_ = _sb_sample(qkv_s, sb_bias[j], cache_sb_k, cache_sb_v, page_table, dec_b, dec_seq, j)
            sbk_s.append(qkv_s[1].reshape(dec_b, dec_seq, SB_HEADS, dh))
            sbv_s.append(qkv_s[2].reshape(dec_b, dec_seq, SB_HEADS, dh))
        elif kind == 1:
            pr_p, pr_s, wo = _in_proj(st_p, hp, hs, gla_w_in, gla_w_out, j, 3)
            g_p = _gla_gate(st_p, hp, gla_w_in, gla_w_gate, gla_b_gate, j)
            g_s = _gla_gate(st_s, hs, gla_w_in, gla_w_gate, gla_b_gate, j)
            s0 = jnp.zeros((batch,) + state_gla.shape[2:], F32)
            op, sp = _gla(pr_p, g_p, gla_norm_g[j], s0, batch, seq)
            os_, ss = _gla(pr_s, g_s, gla_norm_g[j], state_gla[j], dec_b, dec_seq)
            gla_p.append(sp)
            gla_s.append(ss)
        else:
            pr_p, pr_s, wo = _in_proj(st_p, hp, hs, conv_w_in, conv_w_out, j, 3)
            b0 = jnp.zeros((batch, CONV_WIDTH - 1, d), F32)
            op, bp = _conv(pr_p, conv_w[j], b0, batch, seq)
            os_, bs = _conv(pr_s, conv_w[j], state_conv[j], dec_b, dec_seq)
            conv_p.append(bp)
            conv_s.append(bs)
        xp, xs, hp, hs = _out_norm(st_p, st_s, op, os_, wo, xp, xs, l, 1, 1.0, norm_g[l, 2], l, 2)
        if l + 1 < depth:
            xp, xs, hp, hs = ffn_half(xp, xs, hp, hs, l, 2, 1, norm_g[l + 1, 0], l + 1, 0)
        else:
            xp, xs, hp, hs = ffn_half(xp, xs, hp, hs, l, 2, 1, final_norm_g, None, None)

    y_prompt = hp.reshape(batch, seq, d)
    y_sample = hs.reshape(dec_b, dec_seq, d)
    kv_shape = (n_sb, batch, seq, SB_HEADS, dh)
    return (y_prompt, y_sample, k_all.reshape(kv_shape), v_all.reshape(kv_shape), jnp.stack(sbk_s), jnp.stack(sbv_s),
            jnp.stack(gla_p), jnp.stack(gla_s), jnp.stack(conv_p), jnp.stack(conv_s))
```

```python
import functools

import jax
import jax.numpy as jnp
from jax import lax
from jax.experimental import pallas as pl
from jax.experimental.pallas import tpu as pltpu

F32 = jnp.float32
BF16 = jnp.bfloat16

N_MIXERS = 3
N_MOD = 9
SB_HEADS = 16
GLA_HEADS = 4
GLA_RANK = 16
GLA_TAU = 16.0
GLA_CHUNK = 64
GLA_HEADS_PER_STEP = 2
GLA_SEQ_TILE = 1024
CONV_WIDTH = 3
RMS_EPS = 1e-6
LOG2E = 1.4426950408889634

LANES = 128
SUBLANES = 8
VMEM_BYTES_V7X = 64 * 1024 * 1024
VMEM_LIMIT = VMEM_BYTES_V7X - 8 * 1024 * 1024

ROW_TILE = 1024
COL_TILE = 1024
QKV_COL_TILE = 512
FFN_COL_TILE = 512
CAST_ROWS = 512
OUT_ROW_TILE_DEEP = 256
OUT_ROW_TILE = 512
NORM_ROW_TILE = 1024
CONV_ROW_TILE = 512
MOD_COL_TILE = 1024
SB_K_TILE = 256
PAGES_PER_STEP = 8


def _params(*sem):
    return pltpu.CompilerParams(dimension_semantics=sem, vmem_limit_bytes=VMEM_LIMIT)


def _sigmoid(x):
    return 1.0 / (1.0 + jnp.exp(-x))


def _softplus(z):
    return jnp.maximum(z, 0.0) + jnp.log1p(jnp.exp(-jnp.abs(z)))


def _split_bf16(x):
    hi = x.astype(BF16)
    lo = (x - hi.astype(F32)).astype(BF16)
    return hi, lo


def _dot(a, b):
    return jnp.dot(a, b, preferred_element_type=F32)


def _dot_nt(a, b):
    return lax.dot_general(a, b, (((1,), (1,)), ((), ())), preferred_element_type=F32)


def _dot_tn(a, b):
    return lax.dot_general(a, b, (((0,), (0,)), ((), ())), preferred_element_type=F32)


def _mod_kernel(c_ref, w_ref, b_ref, o_ref):
    c = c_ref[...]
    s = (c * _sigmoid(c)).astype(BF16)
    o_ref[...] = _dot(s, w_ref[...].astype(BF16)) + b_ref[...]


def _modulation(c_all, w_mod, b_mod):
    depth, d, n = w_mod.shape
    rows = c_all.shape[0]
    tn = MOD_COL_TILE
    return pl.pallas_call(
        _mod_kernel,
        grid=(depth, n // tn),
        in_specs=[
            pl.BlockSpec((rows, d), lambda l, j: (0, 0)),
            pl.BlockSpec((None, d, tn), lambda l, j: (l, 0, j)),
            pl.BlockSpec((None, 1, tn), lambda l, j: (l, 0, j)),
        ],
        out_specs=pl.BlockSpec((None, rows, tn), lambda l, j: (l, 0, j)),
        out_shape=jax.ShapeDtypeStruct((depth, rows, n), F32),
        compiler_params=_params("arbitrary", "arbitrary"),
        name="modulation",
    )(c_all, w_mod, b_mod.reshape(depth, 1, n))


class _Stream:
    def __init__(self, batch, seq, mod, row_tile, norm_row_tile):
        self.batch, self.seq, self.mod = batch, seq, mod
        self.rows = batch * seq
        self.row_tile, self.norm_row_tile = row_tile, norm_row_tile
        self.per_tile_mod = mod.shape[3] != 1

    def mod_spec(self, layer, chunk, tm, width, col_of, row_of):
        r = self.mod.shape[3]
        if self.per_tile_mod:
            assert r == tm
            group = lambda *g: 0
        else:
            tiles_per_seq = self.seq // tm
            group = lambda *g: row_of(*g) // tiles_per_seq
        return pl.BlockSpec((None, None, None, r, width),
                            lambda *g: (layer, group(*g), chunk, 0, col_of(*g)))


def _norm_mod_kernel(x_ref, g_ref, shift_ref, scale_ref, o_ref):
    x = x_ref[...]
    y = x * lax.rsqrt(jnp.mean(x * x, axis=-1, keepdims=True) + RMS_EPS) * g_ref[...]
    o_ref[...] = (y * (1.0 + scale_ref[...]) + shift_ref[...]).astype(o_ref.dtype)


def _norm_mod(st, x, g, layer, sub):
    m, d = x.shape
    tm = st.norm_row_tile
    row_of = lambda i: i
    col_of = lambda i: 0
    return pl.pallas_call(
        _norm_mod_kernel,
        grid=(m // tm,),
        in_specs=[
            pl.BlockSpec((tm, d), lambda i: (i, 0)),
            pl.BlockSpec((1, d), lambda i: (0, 0)),
            st.mod_spec(layer, 3 * sub, tm, d, col_of, row_of),
            st.mod_spec(layer, 3 * sub + 1, tm, d, col_of, row_of),
        ],
        out_specs=pl.BlockSpec((tm, d), lambda i: (i, 0)),
        out_shape=jax.ShapeDtypeStruct((m, d), BF16),
        compiler_params=_params("arbitrary"),
        name="norm_mod",
    )(x, g.reshape(1, d), st.mod, st.mod)


def _cast_chunk_specs(w_out, layer_j, n_col_steps, n_row_steps):
    k_out, d = w_out.shape[-2:]
    rows = CAST_ROWS // n_row_steps
    n_chunks = k_out // rows
    n_cast_cols = n_chunks // n_row_steps
    assert rows * n_row_steps == CAST_ROWS and rows % 16 == 0
    assert n_cast_cols * CAST_ROWS == k_out and n_cast_cols <= n_col_steps
    chunk = lambda j, i: jnp.minimum(j * n_row_steps + i, n_chunks - 1)
    in_spec = pl.BlockSpec((None, rows, d), lambda j, i: (layer_j, chunk(j, i), 0))
    out_spec = pl.BlockSpec((rows, d), lambda j, i: (chunk(j, i), 0))
    return in_spec, out_spec, jax.ShapeDtypeStruct((k_out, d), BF16), n_cast_cols


def _cast_chunk(wo_ref, wob_ref, n_cast_cols):
    @pl.when(pl.program_id(0) < n_cast_cols)
    def _():
        wob_ref[...] = wo_ref[...].astype(BF16)


def _mm_kernel(a_ref, as_ref, w_ref, wo_ref, o_ref, os_ref, wob_ref, wb_ref, *, n_chunks):
    @pl.when(pl.program_id(1) == 0)
    def _():
        wb_ref[...] = w_ref[...].astype(BF16)
        os_ref[...] = _dot(as_ref[...], wb_ref[...])

    _cast_chunk(wo_ref, wob_ref, n_chunks)
    o_ref[...] = _dot(a_ref[...], wb_ref[...])


def _in_proj(st, a, a_s, w, w_out, layer_j, n_parts):
    m, k = a.shape
    ms = a_s.shape[0]
    tm, tn = st.row_tile, COL_TILE
    tpp = k // tn
    wo_in, wo_out, wo_shape, n_chunks = _cast_chunk_specs(w_out, layer_j, n_parts * tpp, m // tm)
    return pl.pallas_call(
        functools.partial(_mm_kernel, n_chunks=n_chunks),
        grid=(n_parts * tpp, m // tm),
        in_specs=[
            pl.BlockSpec((tm, k), lambda j, i: (i, 0)),
            pl.BlockSpec((ms, k), lambda j, i: (0, 0)),
            pl.BlockSpec((None, k, tn), lambda j, i: (layer_j, 0, j)),
            wo_in,
        ],
        out_specs=[
            pl.BlockSpec((None, tm, tn), lambda j, i: (j // tpp, i, j % tpp)),
            pl.BlockSpec((None, ms, tn), lambda j, i: (j // tpp, 0, j % tpp)),
            wo_out,
        ],
        out_shape=[jax.ShapeDtypeStruct((n_parts, m, k), F32), jax.ShapeDtypeStruct((n_parts, ms, k), F32), wo_shape],
        scratch_shapes=[pltpu.VMEM((k, tn), BF16)],
        compiler_params=_params("arbitrary", "arbitrary"),
        name="in_proj",
    )(a, a_s, w, w_out)


def _qkv_kernel(a_ref, as_ref, w_ref, wo_ref, *refs, tpp, n_chunks, layer_j, creates):
    q_ref, k_ref, v_ref, os_ref, wob_ref, wb_ref = refs[0 if creates else 2:]
    part = pl.program_id(0) // tpp

    @pl.when(pl.program_id(1) == 0)
    def _():
        wb_ref[...] = w_ref[...].astype(BF16)
        os_ref[...] = _dot(as_ref[...], wb_ref[...])

    _cast_chunk(wo_ref, wob_ref, n_chunks)
    y = _dot(a_ref[...], wb_ref[...])

    def store_stacked(ref):
        if creates:
            for l in range(ref.shape[0]):
                ref[l] = y if l == layer_j else jnp.zeros_like(y)
        else:
            ref[...] = y

    @pl.when(part == 0)
    def _():
        q_ref[...] = y

    @pl.when(part == 1)
    def _():
        store_stacked(k_ref)

    @pl.when(part == 2)
    def _():
        store_stacked(v_ref)


def _qkv_proj(st, a, a_s, w, w_out, layer_j, n_layers, k_all, v_all):
    m, k = a.shape
    ms = a_s.shape[0]
    tm, tn = st.row_tile, QKV_COL_TILE
    tpp = k // tn
    ni = m // tm
    creates = k_all is None
    wo_in, wo_out, wo_shape, n_chunks = _cast_chunk_specs(w_out, layer_j, 3 * tpp, ni)

    def rows(j, i, part):
        return jnp.where(j < part * tpp, 0, jnp.where(j < (part + 1) * tpp, i, ni - 1))

    def cols(j, part):
        return jnp.clip(j - part * tpp, 0, tpp - 1)

    def stacked_spec(part):
        if creates:
            return pl.BlockSpec((n_layers, tm, tn), lambda j, i: (0, rows(j, i, part), cols(j, part)))
        return pl.BlockSpec((None, tm, tn), lambda j, i: (layer_j, rows(j, i, part), cols(j, part)))

    stacked = jax.ShapeDtypeStruct((n_layers, m, k), F32)
    any_spec = pl.BlockSpec(memory_space=pl.ANY)
    return pl.pallas_call(
        functools.partial(_qkv_kernel, tpp=tpp, n_chunks=n_chunks, layer_j=layer_j, creates=creates),
        grid=(3 * tpp, ni),
        in_specs=[
            pl.BlockSpec((tm, k), lambda j, i: (i, 0)),
            pl.BlockSpec((ms, k), lambda j, i: (0, 0)),
            pl.BlockSpec((None, k, tn), lambda j, i: (layer_j, 0, j)),
            wo_in,
        ] + ([] if creates else [any_spec, any_spec]),
        out_specs=[
            pl.BlockSpec((tm, tn), lambda j, i: (rows(j, i, 0), cols(j, 0))),
            stacked_spec(1),
            stacked_spec(2),
            pl.BlockSpec((None, ms, tn), lambda j, i: (j // tpp, 0, j % tpp)),
            wo_out,
        ],
        out_shape=[jax.ShapeDtypeStruct((m, k), F32), stacked, stacked, jax.ShapeDtypeStruct((3, ms, k), F32),
                   wo_shape],
        scratch_shapes=[pltpu.VMEM((k, tn), BF16)],
        input_output_aliases={} if creates else {4: 1, 5: 2},
        compiler_params=_params("arbitrary", "arbitrary"),
        name="qkv_proj",
    )(a, a_s, w, w_out, *(() if creates else (k_all, v_all)))


def _swiglu(h, wa, wb):
    a = _dot(h, wa)
    return (a * _sigmoid(a)) * _dot(h, wb)


def _ffn_in_kernel(a_ref, as_ref, wa_ref, wb_ref, wo_ref, o_ref, os_ref, wob_ref, wab_ref, wbb_ref):
    @pl.when(pl.program_id(1) == 0)
    def _():
        wab_ref[...] = wa_ref[...].astype(BF16)
        wbb_ref[...] = wb_ref[...].astype(BF16)
        os_ref[...] = _swiglu(as_ref[...], wab_ref[...], wbb_ref[...]).astype(os_ref.dtype)

    wob_ref[...] = wo_ref[...].astype(BF16)
    o_ref[...] = _swiglu(a_ref[...], wab_ref[...], wbb_ref[...]).astype(o_ref.dtype)


def _ffn_in(st, a, a_s, w, w_out, widx):
    m, k = a.shape
    ms = a_s.shape[0]
    hidden, d = w_out.shape[-2:]
    tm, tn = st.row_tile, FFN_COL_TILE
    nj, ni = hidden // tn, m // tm
    cast_rows = hidden // (nj * ni)
    assert cast_rows * nj * ni == hidden and cast_rows % 16 == 0
    wspec = lambda off: pl.BlockSpec((None, None, k, tn), lambda j, i: widx + (0, j + off))
    return pl.pallas_call(
        _ffn_in_kernel,
        grid=(nj, ni),
        in_specs=[pl.BlockSpec((tm, k), lambda j, i: (i, 0)), pl.BlockSpec((ms, k), lambda j, i: (0, 0)),
                  wspec(0), wspec(nj),
                  pl.BlockSpec((None, None, cast_rows, d), lambda j, i: widx + (j * ni + i, 0))],
        out_specs=[pl.BlockSpec((tm, tn), lambda j, i: (i, j)), pl.BlockSpec((ms, tn), lambda j, i: (0, j)),
                   pl.BlockSpec((cast_rows, d), lambda j, i: (j * ni + i, 0))],
        out_shape=[jax.ShapeDtypeStruct((m, hidden), BF16), jax.ShapeDtypeStruct((ms, hidden), BF16),
                   jax.ShapeDtypeStruct((hidden, d), BF16)],
        scratch_shapes=[pltpu.VMEM((k, tn), BF16), pltpu.VMEM((k, tn), BF16)],
        compiler_params=_params("arbitrary", "arbitrary"),
        name="ffn_in",
    )(a, a_s, w, w, w_out)


def _out_norm_kernel(u_ref, us_ref, w_ref, x_ref, xs_ref, gate_ref, gates_ref, g_ref, shift_ref, shifts_ref,
                     scale_ref, scales_ref, o_ref, os_ref, h_ref, hs_ref, *, coef, final):
    def tile(u_ref, x_ref, gate_ref, shift_ref, scale_ref, o_ref, h_ref):
        gate = gate_ref[...]
        if coef != 1.0:
            gate = coef * gate
        x = x_ref[...] + gate * _dot(u_ref[...], w_ref[...])
        o_ref[...] = x
        y = x * lax.rsqrt(jnp.mean(x * x, axis=-1, keepdims=True) + RMS_EPS) * g_ref[...]
        if not final:
            y = y * (1.0 + scale_ref[...]) + shift_ref[...]
        h_ref[...] = y.astype(h_ref.dtype)

    @pl.when(pl.program_id(0) == 0)
    def _():
        tile(us_ref, xs_ref, gates_ref, shifts_ref, scales_ref, os_ref, hs_ref)

    tile(u_ref, x_ref, gate_ref, shift_ref, scale_ref, o_ref, h_ref)


def _out_norm(st, st_s, u, u_s, w_bf16, x, x_s, layer, sub, coef, g_next, next_layer, next_sub):
    m, k = u.shape
    ms = u_s.shape[0]
    d = x.shape[1]
    final = next_layer is None
    tm = OUT_ROW_TILE if k <= d else OUT_ROW_TILE_DEEP
    row_of, col_of, zero = (lambda i: i), (lambda i: 0), (lambda i: 0)
    nl, ns = (layer, sub) if final else (next_layer, next_sub)

    def mods(chunk_layer, chunk):
        return [st.mod_spec(chunk_layer, chunk, tm, d, col_of, row_of),
                st_s.mod_spec(chunk_layer, chunk, ms, d, col_of, zero)]

    h_dtype = F32 if final else BF16
    return pl.pallas_call(
        functools.partial(_out_norm_kernel, coef=coef, final=final),
        grid=(m // tm,),
        in_specs=[
            pl.BlockSpec((tm, k), lambda i: (i, 0)),
            pl.BlockSpec((ms, k), lambda i: (0, 0)),
            pl.BlockSpec((k, d), lambda i: (0, 0), pipeline_mode=pl.Buffered(1)),
            pl.BlockSpec((tm, d), lambda i: (i, 0)),
            pl.BlockSpec((ms, d), lambda i: (0, 0)),
        ] + mods(layer, 3 * sub + 2) + [pl.BlockSpec((1, d), lambda i: (0, 0))]
          + mods(nl, 3 * ns) + mods(nl, 3 * ns + 1),
        out_specs=[pl.BlockSpec((tm, d), lambda i: (i, 0)), pl.BlockSpec((ms, d), lambda i: (0, 0)),
                   pl.BlockSpec((tm, d), lambda i: (i, 0)), pl.BlockSpec((ms, d), lambda i: (0, 0))],
        out_shape=[jax.ShapeDtypeStruct((m, d), F32), jax.ShapeDtypeStruct((ms, d), F32),
                   jax.ShapeDtypeStruct((m, d), h_dtype), jax.ShapeDtypeStruct((ms, d), h_dtype)],
        compiler_params=_params("arbitrary"),
        name="out_norm",
    )(u, u_s, w_bf16, x, x_s, st.mod, st_s.mod, g_next.reshape(1, d), st.mod, st_s.mod, st.mod, st_s.mod)


def _softplus2(z2):
    neg_abs = pltpu.bitcast(pltpu.bitcast(z2, jnp.uint32) | jnp.uint32(0x80000000), F32)
    return jnp.maximum(z2, 0.0) + jnp.log(1.0 + jnp.exp2(neg_abs)) * LOG2E


def _sb_block(q, kb, vb, carry, mask, *, neg_tri, bias2):
    z = _dot_nt(q, kb) + bias2
    sp = _softplus2(z)
    sp_seen = sp if mask is None else jnp.where(mask, sp, 0.0)
    between = _dot(sp_seen.astype(BF16), neg_tri) + carry
    a = jnp.exp2((z - sp) + between)
    if mask is not None:
        a = jnp.where(mask, a, 0.0)
    new_carry = between[:, 0:1] - sp_seen[:, 0:1]
    return _dot(a.astype(BF16), vb), new_carry


def _sb_prompt_kernel(bias_ref, q_ref, k_ref, v_ref, o_ref, kb_ref, vb_ref, qb_ref, acc_ref, carry_ref,
                      *, tk, scale2):
    kb_ref[...] = k_ref[...].astype(BF16)
    vb_ref[...] = v_ref[...].astype(BF16)
    qb_ref[...] = (q_ref[...] * scale2).astype(BF16)
    tq = 2 * tk
    row = lax.broadcasted_iota(jnp.int32, (tk, tk), 0)
    col = lax.broadcasted_iota(jnp.int32, (tk, tk), 1)
    neg_tri = jnp.where(row > col, -1.0, 0.0).astype(BF16)
    causal = col < row
    block = functools.partial(_sb_block, neg_tri=neg_tri, bias2=bias_ref[pl.program_id(1)] * LOG2E)

    def keys(j):
        start = pl.multiple_of(j * tk, tk)
        return kb_ref[pl.ds(start, tk), :], vb_ref[pl.ds(start, tk), :]

    zero = jnp.zeros((tk, 1), F32)

    def query_tile(i, _):
        q_lo = qb_ref[pl.ds(pl.multiple_of(i * tq, tq), tk), :]
        q_hi = qb_ref[pl.ds(pl.multiple_of(i * tq + tk, tk), tk), :]
        kb, vb = keys(2 * i + 1)
        out_b, carry_b = block(q_hi, kb, vb, zero, causal)
        kb, vb = keys(2 * i)
        out_a, carry_a = block(q_lo, kb, vb, zero, causal)
        out_c, carry_c = block(q_hi, kb, vb, carry_b, None)
        acc_ref[:tk, :] = out_a
        acc_ref[tk:, :] = out_b + out_c
        carry_ref[:tk, :] = carry_a
        carry_ref[tk:, :] = carry_c

        def body(t, _):
            p = i - 1 - t
            q = qb_ref[pl.ds(pl.multiple_of(i * tq, tq), tq), :]
            kb, vb = keys(2 * p + 1)
            out1, carry = block(q, kb, vb, carry_ref[...], None)
            kb, vb = keys(2 * p)
            out2, carry = block(q, kb, vb, carry, None)
            acc_ref[...] += out1 + out2
            carry_ref[...] = carry
            return 0

        lax.fori_loop(0, i, body, 0)
        o_ref[pl.ds(pl.multiple_of(i * tq, tq), tq), :] = acc_ref[...].astype(o_ref.dtype)
        return 0

    lax.fori_loop(0, q_ref.shape[0] // tq, query_tile, 0)


def _sb_prompt(q, k_all, v_all, bias, batch, seq, layer_j):
    m, width = q.shape
    dh = width // SB_HEADS
    tk = SB_K_TILE
    tq = 2 * tk
    assert seq % tq == 0
    kv_spec = pl.BlockSpec((None, None, seq, dh), lambda b, h: (layer_j, b, 0, h))
    stacked = (k_all.shape[0], batch, seq, width)
    out = pl.pallas_call(
        functools.partial(_sb_prompt_kernel, tk=tk, scale2=dh ** -0.5 * LOG2E),
        grid=(batch, SB_HEADS),
        in_specs=[
            pl.BlockSpec(memory_space=pltpu.SMEM),
            pl.BlockSpec((None, seq, dh), lambda b, h: (b, 0, h)),
            kv_spec,
            kv_spec,
        ],
        out_specs=pl.BlockSpec((None, seq, dh), lambda b, h: (b, 0, h)),
        out_shape=jax.ShapeDtypeStruct((batch, seq, width), BF16),
        scratch_shapes=[
            pltpu.VMEM((seq, dh), BF16), pltpu.VMEM((seq, dh), BF16), pltpu.VMEM((seq, dh), BF16),
            pltpu.VMEM((tq, dh), F32), pltpu.VMEM((tq, 1), F32),
        ],
        compiler_params=_params("arbitrary", "arbitrary"),
        name="sb_prompt",
    )(bias, q.reshape(batch, seq, width), k_all.reshape(stacked), v_all.reshape(stacked))
    return out.reshape(m, width)


def _sb_sample_page(q, bias2, k_ref, v_ref, carry, valid, tri, *, scale2):
    n_pairs = SUBLANES
    n_lanes = tri.shape[0]
    rows = q.shape[0] // n_pairs
    z = jnp.concatenate(
        [_dot_nt(q[p * rows:(p + 1) * rows], k_ref[pl.ds(p, n_lanes, stride=n_pairs), :].astype(BF16))
         for p in range(n_pairs)], axis=0) * scale2 + bias2
    sp = _softplus2(z)
    log_not = jnp.where(valid, -sp, 0.0)
    between = _dot(log_not.astype(BF16), tri) + carry
    a = jnp.where(valid, jnp.exp2((z - sp) + between), 0.0).astype(BF16)
    out = jnp.concatenate(
        [_dot(a[p * rows:(p + 1) * rows], v_ref[pl.ds(p, n_lanes, stride=n_pairs), :].astype(BF16))
         for p in range(n_pairs)], axis=0)
    return out, carry + jnp.sum(log_not, axis=1, keepdims=True)


def _sb_sample_kernel(pt_ref, q_ref, bias_ref, knew_ref, vnew_ref, *refs, n_pages_step, n_new, scale2):
    k_refs = refs[:n_pages_step]
    v_refs = refs[n_pages_step:2 * n_pages_step]
    o_ref, acc_ref, carry_ref = refs[2 * n_pages_step:]
    s = pl.program_id(1)
    q = q_ref[...]
    bias2 = bias_ref[...] * LOG2E
    n_rows = q.shape[0]

    def masks(n_lanes):
        r = lax.broadcasted_iota(jnp.int32, (n_rows, n_lanes), 0)
        c = lax.broadcasted_iota(jnp.int32, (n_rows, n_lanes), 1)
        same_head = (r // n_new) % 2 == c % 2
        tr = lax.broadcasted_iota(jnp.int32, (n_lanes, n_lanes), 0)
        tc = lax.broadcasted_iota(jnp.int32, (n_lanes, n_lanes), 1)
        tri = jnp.where((tr // 2 > tc // 2) & (tr % 2 == tc % 2), 1.0, 0.0).astype(BF16)
        return r, c, same_head, tri

    @pl.when(s == 0)
    def _():
        r, c, same_head, tri = masks(knew_ref.shape[0] // SUBLANES)
        valid = same_head & (c // 2 < r % n_new)
        out, carry = _sb_sample_page(q, bias2, knew_ref, vnew_ref, jnp.zeros((n_rows, 1), F32), valid, tri,
                                     scale2=scale2)
        acc_ref[...] = out
        carry_ref[...] = carry

    _, _, same_head, tri = masks(k_refs[0].shape[0] // SUBLANES)
    carry = carry_ref[...]
    acc = acc_ref[...]
    for g in range(n_pages_step):
        out, carry = _sb_sample_page(q, bias2, k_refs[g], v_refs[g], carry, same_head, tri, scale2=scale2)
        acc = acc + out
    acc_ref[...] = acc
    carry_ref[...] = carry

    @pl.when(s == pl.num_programs(1) - 1)
    def _():
        dh = q.shape[1]
        for p in range(SUBLANES):
            for e in range(2):
                h = e * SUBLANES + p
                src = (p * 2 + e) * n_new
                o_ref[:, h * dh:(h + 1) * dh] = acc_ref[src:src + n_new, :].astype(o_ref.dtype)


def _sb_sample(qkv, bias, cache_k, cache_v, page_table, batch, n_new, layer_j):
    _, m, width = qkv.shape
    dh = width // SB_HEADS
    n_layers, n_pool, page = cache_k.shape[:3]
    n_pages = page_table.shape[1]
    g = PAGES_PER_STEP
    assert n_pages % g == 0 and SB_HEADS == 2 * SUBLANES and dh == LANES
    n_rows = SB_HEADS * n_new
    pad_tok = LANES // 2 - n_new
    q = qkv[0].reshape(batch, n_new, 2, SUBLANES, dh).transpose(0, 3, 2, 1, 4).reshape(batch, n_rows, dh)
    bias_rows = jnp.repeat(bias.reshape(2, SUBLANES).T.reshape(-1), n_new).reshape(n_rows, 1)

    def new_rows(part):
        x = qkv[part].reshape(batch, n_new, SB_HEADS, dh)
        return jnp.pad(x, ((0, 0), (0, pad_tok), (0, 0), (0, 0))).reshape(batch, (n_new + pad_tok) * SB_HEADS, dh)

    ck = cache_k.reshape(n_layers, n_pool, page * SB_HEADS, dh)
    cv = cache_v.reshape(n_layers, n_pool, page * SB_HEADS, dh)

    def page_spec(gi):
        return pl.BlockSpec((None, None, page * SB_HEADS, dh),
                            lambda b, s, pt: (layer_j, pt[b, n_pages - 1 - (s * g + gi)], 0, 0))

    new_spec = pl.BlockSpec((None, (n_new + pad_tok) * SB_HEADS, dh), lambda b, s, pt: (b, 0, 0))
    grid_spec = pltpu.PrefetchScalarGridSpec(
        num_scalar_prefetch=1,
        grid=(batch, n_pages // g),
        in_specs=[
            pl.BlockSpec((None, n_rows, dh), lambda b, s, pt: (b, 0, 0)),
            pl.BlockSpec((n_rows, 1), lambda b, s, pt: (0, 0)),
            new_spec,
            new_spec,
        ] + [page_spec(gi) for gi in range(g)] * 2,
        out_specs=pl.BlockSpec((None, n_new, width), lambda b, s, pt: (b, 0, 0)),
        scratch_shapes=[pltpu.VMEM((n_rows, dh), F32), pltpu.VMEM((n_rows, 1), F32)],
    )
    out = pl.pallas_call(
        functools.partial(_sb_sample_kernel, n_pages_step=g, n_new=n_new, scale2=dh ** -0.5 * LOG2E),
        grid_spec=grid_spec,
        out_shape=jax.ShapeDtypeStruct((batch, n_new, width), BF16),
        compiler_params=_params("arbitrary", "arbitrary"),
        name="sb_sample",
    )(page_table, q.astype(BF16), bias_rows, new_rows(1), new_rows(2), *([ck] * g), *([cv] * g))
    return out.reshape(m, width)


def _gla_gate_kernel(h_ref, wr_ref, wg_ref, b_ref, o_ref):
    glr = _dot(h_ref[...], wr_ref[...].astype(BF16))
    x = _dot(glr.astype(BF16), wg_ref[...].astype(BF16)) + b_ref[...]
    o_ref[...] = -_softplus(-x) / GLA_TAU


def _gla_gate(st, h, w_in, w_gate, b_gate, j):
    m, d = h.shape
    dk = w_gate.shape[-1]
    rank = w_gate.shape[1]
    w_r = jnp.pad(w_in[j][:, w_in.shape[-1] - rank:], ((0, 0), (0, LANES - rank)))
    w_g = jnp.pad(w_gate[j], ((0, LANES - rank), (0, 0)))
    tm = st.norm_row_tile
    return pl.pallas_call(
        _gla_gate_kernel,
        grid=(m // tm,),
        in_specs=[
            pl.BlockSpec((tm, d), lambda i: (i, 0)),
            pl.BlockSpec((d, LANES), lambda i: (0, 0)),
            pl.BlockSpec((LANES, dk), lambda i: (0, 0)),
            pl.BlockSpec((1, dk), lambda i: (0, 0)),
        ],
        out_specs=pl.BlockSpec((tm, dk), lambda i: (i, 0)),
        out_shape=jax.ShapeDtypeStruct((m, dk), F32),
        compiler_params=_params("arbitrary"),
        name="gla_gate",
    )(h, w_r, w_g, b_gate[j].reshape(1, dk))


def _gla_kernel(q_ref, k_ref, v_ref, r_ref, g_ref, ng_ref, s0_ref, o_ref, s_ref, *, chunk, n_chunks, scale):
    @pl.when(pl.program_id(2) == 0)
    def _():
        s_ref[...] = s0_ref[...]

    n_heads, dkh, dvh = s_ref.shape
    row = lax.broadcasted_iota(jnp.int32, (chunk, chunk), 0)
    col = lax.broadcasted_iota(jnp.int32, (chunk, chunk), 1)
    keep = col <= row
    lower = jnp.where(keep, 1.0, 0.0).astype(BF16)
    ones = jnp.ones((chunk, LANES), BF16)
    ng = ng_ref[...]

    def head_chunk(hh, rows):
        kcols = slice(hh * dkh, (hh + 1) * dkh)
        vcols = slice(hh * dvh, (hh + 1) * dvh)
        g = g_ref[rows, kcols]
        g_hi, g_lo = _split_bf16(g)
        bcum = _dot(lower, g_hi) + _dot(lower, g_lo)
        b_last = bcum[chunk - 1:chunk, :]
        q = q_ref[rows, kcols] * scale
        k = k_ref[rows, kcols]
        v = v_ref[rows, vcols].astype(BF16)
        qe = (q * jnp.exp(bcum)).astype(BF16)
        ke = (k * jnp.exp(-bcum)).astype(BF16)
        kl = (k * jnp.exp(b_last - bcum)).astype(BF16)
        state = s_ref[hh]
        scores = jnp.where(keep, _dot_nt(qe, ke), 0.0)
        o = _dot(qe, state.astype(BF16)) + _dot(scores.astype(BF16), v)
        total = _dot_tn(g_hi, ones) + _dot_tn(g_lo, ones)
        decay = jnp.exp(total)
        s_ref[hh] = jnp.concatenate([decay] * (dvh // LANES), axis=1) * state + _dot_tn(kl, v)
        on = o * lax.rsqrt(jnp.mean(o * o, axis=-1, keepdims=True) + RMS_EPS) * ng
        r = r_ref[rows, vcols]
        o_ref[rows, vcols] = (on * (r * _sigmoid(r))).astype(o_ref.dtype)

    def body(c, _):
        rows = slice(None) if n_chunks == 1 else pl.ds(pl.multiple_of(c * chunk, chunk), chunk)
        for hh in range(n_heads):
            head_chunk(hh, rows)
        return 0

    if n_chunks == 1:
        body(0, 0)
    else:
        lax.fori_loop(0, n_chunks, body, 0)


def _gla(proj, g, norm_g, s0, batch, seq):
    _, m, d = proj.shape
    dk = g.shape[1]
    dkh, dvh = dk // GLA_HEADS, d // GLA_HEADS
    chunk = min(GLA_CHUNK, seq)
    ts = min(GLA_SEQ_TILE, seq)
    hg = GLA_HEADS_PER_STEP
    n_groups = GLA_HEADS // hg
    p4 = proj.reshape(3, batch, seq, d)
    g3 = g.reshape(batch, seq, dk)
    out, state = pl.pallas_call(
        functools.partial(_gla_kernel, chunk=chunk, n_chunks=ts // chunk, scale=dkh ** -0.5),
        grid=(batch, n_groups, seq // ts),
        in_specs=[
            pl.BlockSpec((None, None, ts, hg * dkh), lambda b, h, t: (0, b, t, h)),
            pl.BlockSpec((None, None, ts, hg * dkh), lambda b, h, t: (0, b, t, n_groups + h)),
            pl.BlockSpec((None, None, ts, hg * dvh), lambda b, h, t: (1, b, t, h)),
            pl.BlockSpec((None, None, ts, hg * dvh), lambda b, h, t: (2, b, t, h)),
            pl.BlockSpec((None, ts, hg * dkh), lambda b, h, t: (b, t, h)),
            pl.BlockSpec((1, dvh), lambda b, h, t: (0, 0)),
            pl.BlockSpec((None, hg, dkh, dvh), lambda b, h, t: (b, h, 0, 0)),
        ],
        out_specs=[
            pl.BlockSpec((None, ts, hg * dvh), lambda b, h, t: (b, t, h)),
            pl.BlockSpec((None, hg, dkh, dvh), lambda b, h, t: (b, h, 0, 0)),
        ],
        out_shape=[
            jax.ShapeDtypeStruct((batch, seq, d), BF16),
            jax.ShapeDtypeStruct((batch, GLA_HEADS, dkh, dvh), F32),
        ],
        compiler_params=_params("arbitrary", "arbitrary", "arbitrary"),
        name="gla",
    )(p4, p4, p4, p4, g3, norm_g.reshape(1, dvh), s0)
    return out.reshape(m, d), state


def _conv_kernel(gb_ref, gc_ref, u_ref, w_ref, buf_ref, o_ref, tail_ref, z_ref, *, tt, n_prev):
    i = pl.program_id(1)

    @pl.when(i == 0)
    def _():
        z_ref[SUBLANES - n_prev:SUBLANES, :] = buf_ref[...]

    z = gc_ref[...] * u_ref[...]
    z_ref[SUBLANES:SUBLANES + tt, :] = z
    conv = w_ref[n_prev:n_prev + 1, :] * z
    for j in range(n_prev):
        conv = conv + w_ref[j:j + 1, :] * z_ref[SUBLANES - n_prev + j:SUBLANES - n_prev + j + tt, :]
    o_ref[...] = (gb_ref[...] * conv).astype(o_ref.dtype)
    last = z_ref[tt:tt + SUBLANES, :]
    z_ref[0:SUBLANES, :] = last
    tail_ref[...] = last[SUBLANES - n_prev:, :]


def _conv(proj, w_conv, buf, batch, seq):
    _, m, d = proj.shape
    n_prev = CONV_WIDTH - 1
    tt = min(seq, CONV_ROW_TILE)
    p4 = proj.reshape(3, batch, seq, d)
    part = lambda p: pl.BlockSpec((None, None, tt, d), lambda b, i: (p, b, i, 0))
    out, tail = pl.pallas_call(
        functools.partial(_conv_kernel, tt=tt, n_prev=n_prev),
        grid=(batch, seq // tt),
        in_specs=[
            part(0), part(1), part(2),
            pl.BlockSpec((CONV_WIDTH, d), lambda b, i: (0, 0)),
            pl.BlockSpec((None, n_prev, d), lambda b, i: (b, 0, 0)),
        ],
        out_specs=[
            pl.BlockSpec((None, tt, d), lambda b, i: (b, i, 0)),
            pl.BlockSpec((None, n_prev, d), lambda b, i: (b, 0, 0)),
        ],
        out_shape=[
            jax.ShapeDtypeStruct((batch, seq, d), BF16),
            jax.ShapeDtypeStruct((batch, n_prev, d), F32),
        ],
        scratch_shapes=[pltpu.VMEM((tt + SUBLANES, d), F32)],
        compiler_params=_params("arbitrary", "arbitrary"),
        name="short_conv",
    )(p4, p4, p4, w_conv, buf)
    return out.reshape(m, d), tail


def kernel(x_prompt, x_sample, c_prompt, c_sample, cache_sb_k, cache_sb_v, page_table, state_gla, state_conv,
           norm_g, w_mod, b_mod, ffn_w_in, ffn_w_out, sb_w_in, sb_w_out, sb_bias, gla_w_in, gla_w_gate, gla_b_gate,
           gla_norm_g, gla_w_out, conv_w_in, conv_w, conv_w_out, final_norm_g):
    batch, seq, d = x_prompt.shape
    dec_b, dec_seq, _ = x_sample.shape
    depth = norm_g.shape[0]
    dh = d // SB_HEADS

    n_seq = batch + dec_b
    pad = (-n_seq) % SUBLANES
    c_all = jnp.concatenate([c_prompt, c_sample, jnp.zeros((pad, d), F32)], axis=0)
    mods = _modulation(c_all, w_mod, b_mod)
    mod_p = mods[:, :batch].reshape(depth, batch, N_MOD, 1, d)
    mod_s = jnp.repeat(mods[:, batch:n_seq], dec_seq, axis=1).reshape(depth, 1, dec_b * dec_seq, N_MOD, d)
    mod_s = mod_s.transpose(0, 1, 3, 2, 4)

    rows_s = dec_b * dec_seq
    st_p = _Stream(batch, seq, mod_p, ROW_TILE, NORM_ROW_TILE)
    st_s = _Stream(dec_b, dec_seq, mod_s, rows_s, rows_s)
    xp = x_prompt.reshape(batch * seq, d)
    xs = x_sample.reshape(rows_s, d)

    def ffn_half(xp, xs, hp, hs, l, sub, which, g_next, next_layer, next_sub):
        up, us, wo = _ffn_in(st_p, hp, hs, ffn_w_in, ffn_w_out, (l, which))
        return _out_norm(st_p, st_s, up, us, wo, xp, xs, l, sub, 0.5, g_next, next_layer, next_sub)

    n_sb = sb_w_in.shape[0]
    k_all = v_all = None
    sbk_s, sbv_s = [], []
    gla_p, gla_s, conv_p, conv_s = [], [], [], []

    hp = _norm_mod(st_p, xp, norm_g[0, 0], 0, 0)
    hs = _norm_mod(st_s, xs, norm_g[0, 0], 0, 0)
    for l in range(depth):
        xp, xs, hp, hs = ffn_half(xp, xs, hp, hs, l, 0, 0, norm_g[l, 1], l, 1)
        kind, j = l % N_MIXERS, l // N_MIXERS
        if kind == 0:
            q_p, k_all, v_all, qkv_s, wo = _qkv_proj(st_p, hp, hs, sb_w_in, sb_w_out, j, n_sb, k_all, v_all)
            op = _sb_prompt(q_p, k_all, v_all, sb_bias[j], batch, seq, j)
            os_ = _sb_sample(qkv_s, sb_bias[j], cache_sb_k, cache_sb_v, page_table, dec_b, dec_seq, j)
            sbk_s.append(qkv_s[1].reshape(dec_b, dec_seq, SB_HEADS, dh))
            sbv_s.append(qkv_s[2].reshape(dec_b, dec_seq, SB_HEADS, dh))
        elif kind == 1:
            pr_p, pr_s, wo = _in_proj(st_p, hp, hs, gla_w_in, gla_w_out, j, 3)
            g_p = _gla_gate(st_p, hp, gla_w_in, gla_w_gate, gla_b_gate, j)
            g_s = _gla_gate(st_s, hs, gla_w_in, gla_w_gate, gla_b_gate, j)
            s0 = jnp.zeros((batch,) + state_gla.shape[2:], F32)
            op, sp = _gla(pr_p, g_p, gla_norm_g[j], s0, batch, seq)
            os_, ss = _gla(pr_s, g_s, gla_norm_g[j], state_gla[j], dec_b, dec_seq)
            gla_p.append(sp)
            gla_s.append(ss)
        else:
            pr_p, pr_s, wo = _in_proj(st_p, hp, hs, conv_w_in, conv_w_out, j, 3)
            b0 = jnp.zeros((batch, CONV_WIDTH - 1, d), F32)
            op, bp = _conv(pr_p, conv_w[j], b0, batch, seq)
            os_, bs = _conv(pr_s, conv_w[j], state_conv[j], dec_b, dec_seq)
            conv_p.append(bp)
            conv_s.append(bs)
        xp, xs, hp, hs = _out_norm(st_p, st_s, op, os_, wo, xp, xs, l, 1, 1.0, norm_g[l, 2], l, 2)
        if l + 1 < depth:
            xp, xs, hp, hs = ffn_half(xp, xs, hp, hs, l, 2, 1, norm_g[l + 1, 0], l + 1, 0)
        else:
            xp, xs, hp, hs = ffn_half(xp, xs, hp, hs, l, 2, 1, final_norm_g, None, None)

    y_prompt = hp.reshape(batch, seq, d)
    y_sample = hs.reshape(dec_b, dec_seq, d)
    kv_shape = (n_sb, batch, seq, SB_HEADS, dh)
    return (y_prompt, y_sample, k_all.reshape(kv_shape), v_all.reshape(kv_shape), jnp.stack(sbk_s), jnp.stack(sbv_s),
            jnp.stack(gla_p), jnp.stack(gla_s), jnp.stack(conv_p), jnp.stack(conv_s))
```

```python
import functools

import jax
import jax.numpy as jnp
from jax import lax
from jax.experimental import pallas as pl
from jax.experimental.pallas import tpu as pltpu

F32 = jnp.float32
BF16 = jnp.bfloat16

N_MIXERS = 3
N_MOD = 9
SB_HEADS = 16
GLA_HEADS = 4
GLA_RANK = 16
GLA_TAU = 16.0
GLA_CHUNK = 64
GLA_HEADS_PER_STEP = 2
GLA_SEQ_TILE = 1024
CONV_WIDTH = 3
RMS_EPS = 1e-6
LOG2E = 1.4426950408889634

LANES = 128
SUBLANES = 8
VMEM_BYTES_V7X = 64 * 1024 * 1024
VMEM_LIMIT = VMEM_BYTES_V7X - 8 * 1024 * 1024

ROW_TILE = 1024
COL_TILE = 1024
FFN_COL_TILE = 512
CAST_ROWS = 512
OUT_ROW_TILE_DEEP = 256
OUT_ROW_TILE = 512
NORM_ROW_TILE = 1024
CONV_ROW_TILE = 512
MOD_COL_TILE = 1024
SB_K_TILE = 256
PAGES_PER_STEP = 8


def _params(*sem):
    return pltpu.CompilerParams(dimension_semantics=sem, vmem_limit_bytes=VMEM_LIMIT)


def _sigmoid(x):
    return 1.0 / (1.0 + jnp.exp(-x))


def _softplus(z):
    return jnp.maximum(z, 0.0) + jnp.log1p(jnp.exp(-jnp.abs(z)))


def _split_bf16(x):
    hi = x.astype(BF16)
    lo = (x - hi.astype(F32)).astype(BF16)
    return hi, lo


def _dot(a, b):
    return jnp.dot(a, b, preferred_element_type=F32)


def _dot_nt(a, b):
    return lax.dot_general(a, b, (((1,), (1,)), ((), ())), preferred_element_type=F32)


def _dot_tn(a, b):
    return lax.dot_general(a, b, (((0,), (0,)), ((), ())), preferred_element_type=F32)


def _mod_kernel(c_ref, w_ref, b_ref, o_ref):
    c = c_ref[...]
    s = (c * _sigmoid(c)).astype(BF16)
    o_ref[...] = _dot(s, w_ref[...].astype(BF16)) + b_ref[...]


def _modulation(c_all, w_mod, b_mod):
    depth, d, n = w_mod.shape
    rows = c_all.shape[0]
    tn = MOD_COL_TILE
    return pl.pallas_call(
        _mod_kernel,
        grid=(depth, n // tn),
        in_specs=[
            pl.BlockSpec((rows, d), lambda l, j: (0, 0)),
            pl.BlockSpec((None, d, tn), lambda l, j: (l, 0, j)),
            pl.BlockSpec((None, 1, tn), lambda l, j: (l, 0, j)),
        ],
        out_specs=pl.BlockSpec((None, rows, tn), lambda l, j: (l, 0, j)),
        out_shape=jax.ShapeDtypeStruct((depth, rows, n), F32),
        compiler_params=_params("arbitrary", "arbitrary"),
        name="modulation",
    )(c_all, w_mod, b_mod.reshape(depth, 1, n))


class _Stream:
    def __init__(self, batch, seq, mod, row_tile, norm_row_tile):
        self.batch, self.seq, self.mod = batch, seq, mod
        self.rows = batch * seq
        self.row_tile, self.norm_row_tile = row_tile, norm_row_tile
        self.per_tile_mod = mod.shape[3] != 1

    def mod_spec(self, layer, chunk, tm, width, col_of, row_of):
        r = self.mod.shape[3]
        if self.per_tile_mod:
            assert r == tm
            group = lambda *g: 0
        else:
            tiles_per_seq = self.seq // tm
            group = lambda *g: row_of(*g) // tiles_per_seq
        return pl.BlockSpec((None, None, None, r, width),
                            lambda *g: (layer, group(*g), chunk, 0, col_of(*g)))


def _norm_mod_kernel(x_ref, g_ref, shift_ref, scale_ref, o_ref):
    x = x_ref[...]
    y = x * lax.rsqrt(jnp.mean(x * x, axis=-1, keepdims=True) + RMS_EPS) * g_ref[...]
    o_ref[...] = (y * (1.0 + scale_ref[...]) + shift_ref[...]).astype(o_ref.dtype)


def _norm_mod(st, x, g, layer, sub):
    m, d = x.shape
    tm = st.norm_row_tile
    row_of = lambda i: i
    col_of = lambda i: 0
    return pl.pallas_call(
        _norm_mod_kernel,
        grid=(m // tm,),
        in_specs=[
            pl.BlockSpec((tm, d), lambda i: (i, 0)),
            pl.BlockSpec((1, d), lambda i: (0, 0)),
            st.mod_spec(layer, 3 * sub, tm, d, col_of, row_of),
            st.mod_spec(layer, 3 * sub + 1, tm, d, col_of, row_of),
        ],
        out_specs=pl.BlockSpec((tm, d), lambda i: (i, 0)),
        out_shape=jax.ShapeDtypeStruct((m, d), BF16),
        compiler_params=_params("arbitrary"),
        name="norm_mod",
    )(x, g.reshape(1, d), st.mod, st.mod)


def _cast_chunk_specs(w_out, layer_j, n_col_steps, n_row_steps):
    k_out, d = w_out.shape[-2:]
    n_cast_cols = min(n_col_steps, k_out // CAST_ROWS)
    n_chunks = n_cast_cols * n_row_steps
    rows = k_out // n_chunks
    assert rows * n_chunks == k_out and rows % 16 == 0
    chunk = lambda j, i: jnp.minimum(j * n_row_steps + i, n_chunks - 1)
    in_spec = pl.BlockSpec((None, rows, d), lambda j, i: (layer_j, chunk(j, i), 0))
    out_spec = pl.BlockSpec((rows, d), lambda j, i: (chunk(j, i), 0))
    return in_spec, out_spec, jax.ShapeDtypeStruct((k_out, d), BF16), n_cast_cols


def _cast_chunk(wo_ref, wob_ref, n_cast_cols):
    @pl.when(pl.program_id(0) < n_cast_cols)
    def _():
        wob_ref[...] = wo_ref[...].astype(BF16)


def _mm_kernel(a_ref, as_ref, w_ref, wo_ref, o_ref, os_ref, wob_ref, wb_ref, *, n_chunks):
    @pl.when(pl.program_id(1) == 0)
    def _():
        wb_ref[...] = w_ref[...].astype(BF16)
        os_ref[...] = _dot(as_ref[...], wb_ref[...])

    _cast_chunk(wo_ref, wob_ref, n_chunks)
    o_ref[...] = _dot(a_ref[...], wb_ref[...])


def _in_proj(st, a, a_s, w, w_out, layer_j, n_parts):
    m, k = a.shape
    ms = a_s.shape[0]
    tm, tn = st.row_tile, COL_TILE
    tpp = k // tn
    wo_in, wo_out, wo_shape, n_chunks = _cast_chunk_specs(w_out, layer_j, n_parts * tpp, m // tm)
    return pl.pallas_call(
        functools.partial(_mm_kernel, n_chunks=n_chunks),
        grid=(n_parts * tpp, m // tm),
        in_specs=[
            pl.BlockSpec((tm, k), lambda j, i: (i, 0)),
            pl.BlockSpec((ms, k), lambda j, i: (0, 0)),
            pl.BlockSpec((None, k, tn), lambda j, i: (layer_j, 0, j)),
            wo_in,
        ],
        out_specs=[
            pl.BlockSpec((None, tm, tn), lambda j, i: (j // tpp, i, j % tpp)),
            pl.BlockSpec((None, ms, tn), lambda j, i: (j // tpp, 0, j % tpp)),
            wo_out,
        ],
        out_shape=[jax.ShapeDtypeStruct((n_parts, m, k), F32), jax.ShapeDtypeStruct((n_parts, ms, k), F32), wo_shape],
        scratch_shapes=[pltpu.VMEM((k, tn), BF16)],
        compiler_params=_params("arbitrary", "arbitrary"),
        name="in_proj",
    )(a, a_s, w, w_out)


def _stacked_kernel(a_ref, as_ref, w_ref, *refs, layer_j, creates):
    o_ref, os_ref, wb_ref = refs[0 if creates else 1:]

    @pl.when(pl.program_id(1) == 0)
    def _():
        wb_ref[...] = w_ref[...].astype(BF16)
        os_ref[...] = _dot(as_ref[...], wb_ref[...])

    y = _dot(a_ref[...], wb_ref[...])
    if creates:
        for l in range(o_ref.shape[0]):
            o_ref[l] = y if l == layer_j else jnp.zeros_like(y)
    else:
        o_ref[...] = y


def _stacked_proj(st, a, a_s, w, layer_j, part, n_layers, stacked):
    m, k = a.shape
    ms = a_s.shape[0]
    tm, tn = st.row_tile, COL_TILE
    tpp = k // tn
    creates = stacked is None
    if creates:
        out_spec = pl.BlockSpec((n_layers, tm, tn), lambda j, i: (0, i, j))
    else:
        out_spec = pl.BlockSpec((None, tm, tn), lambda j, i: (layer_j, i, j))
    return pl.pallas_call(
        functools.partial(_stacked_kernel, layer_j=layer_j, creates=creates),
        grid=(tpp, m // tm),
        in_specs=[
            pl.BlockSpec((tm, k), lambda j, i: (i, 0)),
            pl.BlockSpec((ms, k), lambda j, i: (0, 0)),
            pl.BlockSpec((None, k, tn), lambda j, i: (layer_j, 0, part * tpp + j)),
        ] + ([] if creates else [pl.BlockSpec(memory_space=pl.ANY)]),
        out_specs=[out_spec, pl.BlockSpec((ms, tn), lambda j, i: (0, j))],
        out_shape=[jax.ShapeDtypeStruct((n_layers, m, k), F32), jax.ShapeDtypeStruct((ms, k), F32)],
        scratch_shapes=[pltpu.VMEM((k, tn), BF16)],
        input_output_aliases={} if creates else {3: 0},
        compiler_params=_params("arbitrary", "arbitrary"),
        name="stacked_proj",
    )(a, a_s, w, *(() if creates else (stacked,)))


def _swiglu(h, wa, wb):
    a = _dot(h, wa)
    return (a * _sigmoid(a)) * _dot(h, wb)


def _ffn_in_kernel(a_ref, as_ref, wa_ref, wb_ref, wo_ref, o_ref, os_ref, wob_ref, wab_ref, wbb_ref):
    @pl.when(pl.program_id(1) == 0)
    def _():
        wab_ref[...] = wa_ref[...].astype(BF16)
        wbb_ref[...] = wb_ref[...].astype(BF16)
        os_ref[...] = _swiglu(as_ref[...], wab_ref[...], wbb_ref[...]).astype(os_ref.dtype)

    wob_ref[...] = wo_ref[...].astype(BF16)
    o_ref[...] = _swiglu(a_ref[...], wab_ref[...], wbb_ref[...]).astype(o_ref.dtype)


def _ffn_in(st, a, a_s, w, w_out, widx):
    m, k = a.shape
    ms = a_s.shape[0]
    hidden, d = w_out.shape[-2:]
    tm, tn = st.row_tile, FFN_COL_TILE
    nj, ni = hidden // tn, m // tm
    cast_rows = hidden // (nj * ni)
    assert cast_rows * nj * ni == hidden and cast_rows % 16 == 0
    wspec = lambda off: pl.BlockSpec((None, None, k, tn), lambda j, i: widx + (0, j + off))
    return pl.pallas_call(
        _ffn_in_kernel,
        grid=(nj, ni),
        in_specs=[pl.BlockSpec((tm, k), lambda j, i: (i, 0)), pl.BlockSpec((ms, k), lambda j, i: (0, 0)),
                  wspec(0), wspec(nj),
                  pl.BlockSpec((None, None, cast_rows, d), lambda j, i: widx + (j * ni + i, 0))],
        out_specs=[pl.BlockSpec((tm, tn), lambda j, i: (i, j)), pl.BlockSpec((ms, tn), lambda j, i: (0, j)),
                   pl.BlockSpec((cast_rows, d), lambda j, i: (j * ni + i, 0))],
        out_shape=[jax.ShapeDtypeStruct((m, hidden), BF16), jax.ShapeDtypeStruct((ms, hidden), BF16),
                   jax.ShapeDtypeStruct((hidden, d), BF16)],
        scratch_shapes=[pltpu.VMEM((k, tn), BF16), pltpu.VMEM((k, tn), BF16)],
        compiler_params=_params("arbitrary", "arbitrary"),
        name="ffn_in",
    )(a, a_s, w, w, w_out)


def _out_norm_kernel(u_ref, us_ref, w_ref, x_ref, xs_ref, gate_ref, gates_ref, g_ref, shift_ref, shifts_ref,
                     scale_ref, scales_ref, o_ref, os_ref, h_ref, hs_ref, *, coef, final):
    def tile(u_ref, x_ref, gate_ref, shift_ref, scale_ref, o_ref, h_ref):
        gate = gate_ref[...]
        if coef != 1.0:
            gate = coef * gate
        x = x_ref[...] + gate * _dot(u_ref[...], w_ref[...])
        o_ref[...] = x
        y = x * lax.rsqrt(jnp.mean(x * x, axis=-1, keepdims=True) + RMS_EPS) * g_ref[...]
        if not final:
            y = y * (1.0 + scale_ref[...]) + shift_ref[...]
        h_ref[...] = y.astype(h_ref.dtype)

    @pl.when(pl.program_id(0) == 0)
    def _():
        tile(us_ref, xs_ref, gates_ref, shifts_ref, scales_ref, os_ref, hs_ref)

    tile(u_ref, x_ref, gate_ref, shift_ref, scale_ref, o_ref, h_ref)


def _out_norm(st, st_s, u, u_s, w_bf16, x, x_s, layer, sub, coef, g_next, next_layer, next_sub):
    m, k = u.shape
    ms = u_s.shape[0]
    d = x.shape[1]
    final = next_layer is None
    tm = OUT_ROW_TILE if k <= d else OUT_ROW_TILE_DEEP
    row_of, col_of, zero = (lambda i: i), (lambda i: 0), (lambda i: 0)
    nl, ns = (layer, sub) if final else (next_layer, next_sub)

    def mods(chunk_layer, chunk):
        return [st.mod_spec(chunk_layer, chunk, tm, d, col_of, row_of),
                st_s.mod_spec(chunk_layer, chunk, ms, d, col_of, zero)]

    h_dtype = F32 if final else BF16
    return pl.pallas_call(
        functools.partial(_out_norm_kernel, coef=coef, final=final),
        grid=(m // tm,),
        in_specs=[
            pl.BlockSpec((tm, k), lambda i: (i, 0)),
            pl.BlockSpec((ms, k), lambda i: (0, 0)),
            pl.BlockSpec((k, d), lambda i: (0, 0), pipeline_mode=pl.Buffered(1)),
            pl.BlockSpec((tm, d), lambda i: (i, 0)),
            pl.BlockSpec((ms, d), lambda i: (0, 0)),
        ] + mods(layer, 3 * sub + 2) + [pl.BlockSpec((1, d), lambda i: (0, 0))]
          + mods(nl, 3 * ns) + mods(nl, 3 * ns + 1),
        out_specs=[pl.BlockSpec((tm, d), lambda i: (i, 0)), pl.BlockSpec((ms, d), lambda i: (0, 0)),
                   pl.BlockSpec((tm, d), lambda i: (i, 0)), pl.BlockSpec((ms, d), lambda i: (0, 0))],
        out_shape=[jax.ShapeDtypeStruct((m, d), F32), jax.ShapeDtypeStruct((ms, d), F32),
                   jax.ShapeDtypeStruct((m, d), h_dtype), jax.ShapeDtypeStruct((ms, d), h_dtype)],
        compiler_params=_params("arbitrary"),
        name="out_norm",
    )(u, u_s, w_bf16, x, x_s, st.mod, st_s.mod, g_next.reshape(1, d), st.mod, st_s.mod, st.mod, st_s.mod)


def _softplus2(z2):
    return jnp.maximum(z2, 0.0) + jnp.log(1.0 + jnp.exp2(-jnp.abs(z2))) * LOG2E


def _sb_block(q, kb, vb, carry, mask, *, neg_tri, bias2):
    z = _dot_nt(q, kb) + bias2
    sp = _softplus2(z)
    sp_seen = sp if mask is None else jnp.where(mask, sp, 0.0)
    between = _dot(sp_seen.astype(BF16), neg_tri) + carry
    a = jnp.exp2((z - sp) + between)
    if mask is not None:
        a = jnp.where(mask, a, 0.0)
    new_carry = between[:, 0:1] - sp_seen[:, 0:1]
    return _dot(a.astype(BF16), vb), new_carry


def _sb_prompt_kernel(bias_ref, q_ref, k_ref, v_ref, o_ref, kb_ref, vb_ref, qb_ref, acc_ref, carry_ref,
                      *, tk, scale2):
    kb_ref[...] = k_ref[...].astype(BF16)
    vb_ref[...] = v_ref[...].astype(BF16)
    qb_ref[...] = (q_ref[...] * scale2).astype(BF16)
    tq = 2 * tk
    row = lax.broadcasted_iota(jnp.int32, (tk, tk), 0)
    col = lax.broadcasted_iota(jnp.int32, (tk, tk), 1)
    neg_tri = jnp.where(row > col, -1.0, 0.0).astype(BF16)
    causal = col < row
    block = functools.partial(_sb_block, neg_tri=neg_tri, bias2=bias_ref[pl.program_id(1)] * LOG2E)

    def keys(j):
        start = pl.multiple_of(j * tk, tk)
        return kb_ref[pl.ds(start, tk), :], vb_ref[pl.ds(start, tk), :]

    zero = jnp.zeros((tk, 1), F32)

    def query_tile(i, _):
        q_lo = qb_ref[pl.ds(pl.multiple_of(i * tq, tq), tk), :]
        q_hi = qb_ref[pl.ds(pl.multiple_of(i * tq + tk, tk), tk), :]
        kb, vb = keys(2 * i + 1)
        out_b, carry_b = block(q_hi, kb, vb, zero, causal)
        kb, vb = keys(2 * i)
        out_a, carry_a = block(q_lo, kb, vb, zero, causal)
        out_c, carry_c = block(q_hi, kb, vb, carry_b, None)
        acc_ref[:tk, :] = out_a
        acc_ref[tk:, :] = out_b + out_c
        carry_ref[:tk, :] = carry_a
        carry_ref[tk:, :] = carry_c

        def body(t, _):
            p = i - 1 - t
            q = qb_ref[pl.ds(pl.multiple_of(i * tq, tq), tq), :]
            kb, vb = keys(2 * p + 1)
            out1, carry = block(q, kb, vb, carry_ref[...], None)
            kb, vb = keys(2 * p)
            out2, carry = block(q, kb, vb, carry, None)
            acc_ref[...] += out1 + out2
            carry_ref[...] = carry
            return 0

        lax.fori_loop(0, i, body, 0)
        o_ref[pl.ds(pl.multiple_of(i * tq, tq), tq), :] = acc_ref[...].astype(o_ref.dtype)
        return 0

    lax.fori_loop(0, q_ref.shape[0] // tq, query_tile, 0)


def _sb_prompt(q, k_all, v_all, bias, batch, seq, layer_j):
    m, width = q.shape
    dh = width // SB_HEADS
    tk = SB_K_TILE
    tq = 2 * tk
    assert seq % tq == 0
    kv_spec = pl.BlockSpec((None, None, seq, dh), lambda b, h: (layer_j, b, 0, h))
    stacked = (k_all.shape[0], batch, seq, width)
    out = pl.pallas_call(
        functools.partial(_sb_prompt_kernel, tk=tk, scale2=dh ** -0.5 * LOG2E),
        grid=(batch, SB_HEADS),
        in_specs=[
            pl.BlockSpec(memory_space=pltpu.SMEM),
            pl.BlockSpec((None, seq, dh), lambda b, h: (b, 0, h)),
            kv_spec,
            kv_spec,
        ],
        out_specs=pl.BlockSpec((None, seq, dh), lambda b, h: (b, 0, h)),
        out_shape=jax.ShapeDtypeStruct((batch, seq, width), BF16),
        scratch_shapes=[
            pltpu.VMEM((seq, dh), BF16), pltpu.VMEM((seq, dh), BF16), pltpu.VMEM((seq, dh), BF16),
            pltpu.VMEM((tq, dh), F32), pltpu.VMEM((tq, 1), F32),
        ],
        compiler_params=_params("arbitrary", "arbitrary"),
        name="sb_prompt",
    )(bias, q.reshape(batch, seq, width), k_all.reshape(stacked), v_all.reshape(stacked))
    return out.reshape(m, width)


def _sb_sample_page(q, bias2, k_ref, v_ref, carry, valid, tri, *, scale2):
    n_pairs = SUBLANES
    n_lanes = tri.shape[0]
    rows = q.shape[0] // n_pairs
    z = jnp.concatenate(
        [_dot_nt(q[p * rows:(p + 1) * rows], k_ref[pl.ds(p, n_lanes, stride=n_pairs), :].astype(BF16))
         for p in range(n_pairs)], axis=0) * scale2 + bias2
    sp = _softplus2(z)
    log_not = jnp.where(valid, -sp, 0.0)
    between = _dot(log_not.astype(BF16), tri) + carry
    a = jnp.where(valid, jnp.exp2((z - sp) + between), 0.0).astype(BF16)
    out = jnp.concatenate(
        [_dot(a[p * rows:(p + 1) * rows], v_ref[pl.ds(p, n_lanes, stride=n_pairs), :].astype(BF16))
         for p in range(n_pairs)], axis=0)
    return out, carry + jnp.sum(log_not, axis=1, keepdims=True)


def _sb_sample_kernel(pt_ref, q_ref, bias_ref, knew_ref, vnew_ref, *refs, n_pages_step, n_new, scale2):
    k_refs = refs[:n_pages_step]
    v_refs = refs[n_pages_step:2 * n_pages_step]
    o_ref, acc_ref, carry_ref = refs[2 * n_pages_step:]
    s = pl.program_id(1)
    q = q_ref[...]
    bias2 = bias_ref[...] * LOG2E
    n_rows = q.shape[0]

    def masks(n_lanes):
        r = lax.broadcasted_iota(jnp.int32, (n_rows, n_lanes), 0)
        c = lax.broadcasted_iota(jnp.int32, (n_rows, n_lanes), 1)
        same_head = (r // n_new) % 2 == c % 2
        tr = lax.broadcasted_iota(jnp.int32, (n_lanes, n_lanes), 0)
        tc = lax.broadcasted_iota(jnp.int32, (n_lanes, n_lanes), 1)
        tri = jnp.where((tr // 2 > tc // 2) & (tr % 2 == tc % 2), 1.0, 0.0).astype(BF16)
        return r, c, same_head, tri

    @pl.when(s == 0)
    def _():
        r, c, same_head, tri = masks(knew_ref.shape[0] // SUBLANES)
        valid = same_head & (c // 2 < r % n_new)
        out, carry = _sb_sample_page(q, bias2, knew_ref, vnew_ref, jnp.zeros((n_rows, 1), F32), valid, tri,
                                     scale2=scale2)
        acc_ref[...] = out
        carry_ref[...] = carry

    _, _, same_head, tri = masks(k_refs[0].shape[0] // SUBLANES)
    carry = carry_ref[...]
    acc = acc_ref[...]
    for g in range(n_pages_step):
        out, carry = _sb_sample_page(q, bias2, k_refs[g], v_refs[g], carry, same_head, tri, scale2=scale2)
        acc = acc + out
    acc_ref[...] = acc
    carry_ref[...] = carry

    @pl.when(s == pl.num_programs(1) - 1)
    def _():
        dh = q.shape[1]
        for p in range(SUBLANES):
            for e in range(2):
                h = e * SUBLANES + p
                src = (p * 2 + e) * n_new
                o_ref[:, h * dh:(h + 1) * dh] = acc_ref[src:src + n_new, :].astype(o_ref.dtype)


def _sb_sample(qkv, bias, cache_k, cache_v, page_table, batch, n_new, layer_j):
    _, m, width = qkv.shape
    dh = width // SB_HEADS
    n_layers, n_pool, page = cache_k.shape[:3]
    n_pages = page_table.shape[1]
    g = PAGES_PER_STEP
    assert n_pages % g == 0 and SB_HEADS == 2 * SUBLANES and dh == LANES
    n_rows = SB_HEADS * n_new
    pad_tok = LANES // 2 - n_new
    q = qkv[0].reshape(batch, n_new, 2, SUBLANES, dh).transpose(0, 3, 2, 1, 4).reshape(batch, n_rows, dh)
    bias_rows = jnp.repeat(bias.reshape(2, SUBLANES).T.reshape(-1), n_new).reshape(n_rows, 1)

    def new_rows(part):
        x = qkv[part].reshape(batch, n_new, SB_HEADS, dh)
        return jnp.pad(x, ((0, 0), (0, pad_tok), (0, 0), (0, 0))).reshape(batch, (n_new + pad_tok) * SB_HEADS, dh)

    ck = cache_k.reshape(n_layers, n_pool, page * SB_HEADS, dh)
    cv = cache_v.reshape(n_layers, n_pool, page * SB_HEADS, dh)

    def page_spec(gi):
        return pl.BlockSpec((None, None, page * SB_HEADS, dh),
                            lambda b, s, pt: (layer_j, pt[b, n_pages - 1 - (s * g + gi)], 0, 0))

    new_spec = pl.BlockSpec((None, (n_new + pad_tok) * SB_HEADS, dh), lambda b, s, pt: (b, 0, 0))
    grid_spec = pltpu.PrefetchScalarGridSpec(
        num_scalar_prefetch=1,
        grid=(batch, n_pages // g),
        in_specs=[
            pl.BlockSpec((None, n_rows, dh), lambda b, s, pt: (b, 0, 0)),
            pl.BlockSpec((n_rows, 1), lambda b, s, pt: (0, 0)),
            new_spec,
            new_spec,
        ] + [page_spec(gi) for gi in range(g)] * 2,
        out_specs=pl.BlockSpec((None, n_new, width), lambda b, s, pt: (b, 0, 0)),
        scratch_shapes=[pltpu.VMEM((n_rows, dh), F32), pltpu.VMEM((n_rows, 1), F32)],
    )
    out = pl.pallas_call(
        functools.partial(_sb_sample_kernel, n_pages_step=g, n_new=n_new, scale2=dh ** -0.5 * LOG2E),
        grid_spec=grid_spec,
        out_shape=jax.ShapeDtypeStruct((batch, n_new, width), BF16),
        compiler_params=_params("arbitrary", "arbitrary"),
        name="sb_sample",
    )(page_table, q.astype(BF16), bias_rows, new_rows(1), new_rows(2), *([ck] * g), *([cv] * g))
    return out.reshape(m, width)


def _gla_gate_kernel(h_ref, wr_ref, wg_ref, b_ref, o_ref):
    glr = _dot(h_ref[...], wr_ref[...].astype(BF16))
    x = _dot(glr.astype(BF16), wg_ref[...].astype(BF16)) + b_ref[...]
    o_ref[...] = -_softplus(-x) / GLA_TAU


def _gla_gate(st, h, w_in, w_gate, b_gate, j):
    m, d = h.shape
    dk = w_gate.shape[-1]
    rank = w_gate.shape[1]
    w_r = jnp.pad(w_in[j][:, w_in.shape[-1] - rank:], ((0, 0), (0, LANES - rank)))
    w_g = jnp.pad(w_gate[j], ((0, LANES - rank), (0, 0)))
    tm = st.norm_row_tile
    return pl.pallas_call(
        _gla_gate_kernel,
        grid=(m // tm,),
        in_specs=[
            pl.BlockSpec((tm, d), lambda i: (i, 0)),
            pl.BlockSpec((d, LANES), lambda i: (0, 0)),
            pl.BlockSpec((LANES, dk), lambda i: (0, 0)),
            pl.BlockSpec((1, dk), lambda i: (0, 0)),
        ],
        out_specs=pl.BlockSpec((tm, dk), lambda i: (i, 0)),
        out_shape=jax.ShapeDtypeStruct((m, dk), F32),
        compiler_params=_params("arbitrary"),
        name="gla_gate",
    )(h, w_r, w_g, b_gate[j].reshape(1, dk))


def _gla_kernel(q_ref, k_ref, v_ref, r_ref, g_ref, ng_ref, s0_ref, o_ref, s_ref, *, chunk, n_chunks, scale):
    @pl.when(pl.program_id(2) == 0)
    def _():
        s_ref[...] = s0_ref[...]

    n_heads, dkh, dvh = s_ref.shape
    row = lax.broadcasted_iota(jnp.int32, (chunk, chunk), 0)
    col = lax.broadcasted_iota(jnp.int32, (chunk, chunk), 1)
    keep = col <= row
    lower = jnp.where(keep, 1.0, 0.0).astype(BF16)
    ones = jnp.ones((chunk, LANES), BF16)
    ng = ng_ref[...]

    def head_chunk(hh, rows):
        kcols = slice(hh * dkh, (hh + 1) * dkh)
        vcols = slice(hh * dvh, (hh + 1) * dvh)
        g = g_ref[rows, kcols]
        g_hi, g_lo = _split_bf16(g)
        bcum = _dot(lower, g_hi) + _dot(lower, g_lo)
        b_last = bcum[chunk - 1:chunk, :]
        q = q_ref[rows, kcols] * scale
        k = k_ref[rows, kcols]
        v = v_ref[rows, vcols].astype(BF16)
        qe = (q * jnp.exp(bcum)).astype(BF16)
        ke = (k * jnp.exp(-bcum)).astype(BF16)
        kl = (k * jnp.exp(b_last - bcum)).astype(BF16)
        state = s_ref[hh]
        scores = jnp.where(keep, _dot_nt(qe, ke), 0.0)
        o = _dot(qe, state.astype(BF16)) + _dot(scores.astype(BF16), v)
        total = _dot_tn(g_hi, ones) + _dot_tn(g_lo, ones)
        decay = jnp.exp(total)
        s_ref[hh] = jnp.concatenate([decay] * (dvh // LANES), axis=1) * state + _dot_tn(kl, v)
        on = o * lax.rsqrt(jnp.mean(o * o, axis=-1, keepdims=True) + RMS_EPS) * ng
        r = r_ref[rows, vcols]
        o_ref[rows, vcols] = (on * (r * _sigmoid(r))).astype(o_ref.dtype)

    def body(c, _):
        rows = slice(None) if n_chunks == 1 else pl.ds(pl.multiple_of(c * chunk, chunk), chunk)
        for hh in range(n_heads):
            head_chunk(hh, rows)
        return 0

    if n_chunks == 1:
        body(0, 0)
    else:
        lax.fori_loop(0, n_chunks, body, 0)


def _gla(proj, g, norm_g, s0, batch, seq):
    _, m, d = proj.shape
    dk = g.shape[1]
    dkh, dvh = dk // GLA_HEADS, d // GLA_HEADS
    chunk = min(GLA_CHUNK, seq)
    ts = min(GLA_SEQ_TILE, seq)
    hg = GLA_HEADS_PER_STEP
    n_groups = GLA_HEADS // hg
    p4 = proj.reshape(3, batch, seq, d)
    g3 = g.reshape(batch, seq, dk)
    out, state = pl.pallas_call(
        functools.partial(_gla_kernel, chunk=chunk, n_chunks=ts // chunk, scale=dkh ** -0.5),
        grid=(batch, n_groups, seq // ts),
        in_specs=[
            pl.BlockSpec((None, None, ts, hg * dkh), lambda b, h, t: (0, b, t, h)),
            pl.BlockSpec((None, None, ts, hg * dkh), lambda b, h, t: (0, b, t, n_groups + h)),
            pl.BlockSpec((None, None, ts, hg * dvh), lambda b, h, t: (1, b, t, h)),
            pl.BlockSpec((None, None, ts, hg * dvh), lambda b, h, t: (2, b, t, h)),
            pl.BlockSpec((None, ts, hg * dkh), lambda b, h, t: (b, t, h)),
            pl.BlockSpec((1, dvh), lambda b, h, t: (0, 0)),
            pl.BlockSpec((None, hg, dkh, dvh), lambda b, h, t: (b, h, 0, 0)),
        ],
        out_specs=[
            pl.BlockSpec((None, ts, hg * dvh), lambda b, h, t: (b, t, h)),
            pl.BlockSpec((None, hg, dkh, dvh), lambda b, h, t: (b, h, 0, 0)),
        ],
        out_shape=[
            jax.ShapeDtypeStruct((batch, seq, d), BF16),
            jax.ShapeDtypeStruct((batch, GLA_HEADS, dkh, dvh), F32),
        ],
        compiler_params=_params("arbitrary", "arbitrary", "arbitrary"),
        name="gla",
    )(p4, p4, p4, p4, g3, norm_g.reshape(1, dvh), s0)
    return out.reshape(m, d), state


def _conv_kernel(gb_ref, gc_ref, u_ref, w_ref, buf_ref, o_ref, tail_ref, z_ref, *, tt, n_prev):
    i = pl.program_id(1)

    @pl.when(i == 0)
    def _():
        z_ref[SUBLANES - n_prev:SUBLANES, :] = buf_ref[...]

    z = gc_ref[...] * u_ref[...]
    z_ref[SUBLANES:SUBLANES + tt, :] = z
    conv = w_ref[n_prev:n_prev + 1, :] * z
    for j in range(n_prev):
        conv = conv + w_ref[j:j + 1, :] * z_ref[SUBLANES - n_prev + j:SUBLANES - n_prev + j + tt, :]
    o_ref[...] = (gb_ref[...] * conv).astype(o_ref.dtype)
    last = z_ref[tt:tt + SUBLANES, :]
    z_ref[0:SUBLANES, :] = last
    tail_ref[...] = last[SUBLANES - n_prev:, :]


def _conv(proj, w_conv, buf, batch, seq):
    _, m, d = proj.shape
    n_prev = CONV_WIDTH - 1
    tt = min(seq, CONV_ROW_TILE)
    p4 = proj.reshape(3, batch, seq, d)
    part = lambda p: pl.BlockSpec((None, None, tt, d), lambda b, i: (p, b, i, 0))
    out, tail = pl.pallas_call(
        functools.partial(_conv_kernel, tt=tt, n_prev=n_prev),
        grid=(batch, seq // tt),
        in_specs=[
            part(0), part(1), part(2),
            pl.BlockSpec((CONV_WIDTH, d), lambda b, i: (0, 0)),
            pl.BlockSpec((None, n_prev, d), lambda b, i: (b, 0, 0)),
        ],
        out_specs=[
            pl.BlockSpec((None, tt, d), lambda b, i: (b, i, 0)),
            pl.BlockSpec((None, n_prev, d), lambda b, i: (b, 0, 0)),
        ],
        out_shape=[
            jax.ShapeDtypeStruct((batch, seq, d), BF16),
            jax.ShapeDtypeStruct((batch, n_prev, d), F32),
        ],
        scratch_shapes=[pltpu.VMEM((tt + SUBLANES, d), F32)],
        compiler_params=_params("arbitrary", "arbitrary"),
        name="short_conv",
    )(p4, p4, p4, w_conv, buf)
    return out.reshape(m, d), tail


def kernel(x_prompt, x_sample, c_prompt, c_sample, cache_sb_k, cache_sb_v, page_table, state_gla, state_conv,
           norm_g, w_mod, b_mod, ffn_w_in, ffn_w_out, sb_w_in, sb_w_out, sb_bias, gla_w_in, gla_w_gate, gla_b_gate,
           gla_norm_g, gla_w_out, conv_w_in, conv_w, conv_w_out, final_norm_g):
    batch, seq, d = x_prompt.shape
    dec_b, dec_seq, _ = x_sample.shape
    depth = norm_g.shape[0]
    dh = d // SB_HEADS

    n_seq = batch + dec_b
    pad = (-n_seq) % SUBLANES
    c_all = jnp.concatenate([c_prompt, c_sample, jnp.zeros((pad, d), F32)], axis=0)
    mods = _modulation(c_all, w_mod, b_mod)
    mod_p = mods[:, :batch].reshape(depth, batch, N_MOD, 1, d)
    mod_s = jnp.repeat(mods[:, batch:n_seq], dec_seq, axis=1).reshape(depth, 1, dec_b * dec_seq, N_MOD, d)
    mod_s = mod_s.transpose(0, 1, 3, 2, 4)

    rows_s = dec_b * dec_seq
    st_p = _Stream(batch, seq, mod_p, ROW_TILE, NORM_ROW_TILE)
    st_s = _Stream(dec_b, dec_seq, mod_s, rows_s, rows_s)
    xp = x_prompt.reshape(batch * seq, d)
    xs = x_sample.reshape(rows_s, d)

    def ffn_half(xp, xs, hp, hs, l, sub, which, g_next, next_layer, next_sub):
        up, us, wo = _ffn_in(st_p, hp, hs, ffn_w_in, ffn_w_out, (l, which))
        return _out_norm(st_p, st_s, up, us, wo, xp, xs, l, sub, 0.5, g_next, next_layer, next_sub)

    n_sb = sb_w_in.shape[0]
    k_all = v_all = None
    sbk_s, sbv_s = [], []
    gla_p, gla_s, conv_p, conv_s = [], [], [], []

    hp = _norm_mod(st_p, xp, norm_g[0, 0], 0, 0)
    hs = _norm_mod(st_s, xs, norm_g[0, 0], 0, 0)
    for l in range(depth):
        xp, xs, hp, hs = ffn_half(xp, xs, hp, hs, l, 0, 0, norm_g[l, 1], l, 1)
        kind, j = l % N_MIXERS, l // N_MIXERS
        if kind == 0:
            q_p, q_s, wo = _in_proj(st_p, hp, hs, sb_w_in, sb_w_out, j, 1)
            k_all, k_s = _stacked_proj(st_p, hp, hs, sb_w_in, j, 1, n_sb, k_all)
            v_all, v_s = _stacked_proj(st_p, hp, hs, sb_w_in, j, 2, n_sb, v_all)
            qkv_s = jnp.stack([q_s[0], k_s, v_s])
            op = _sb_prompt(q_p[0], k_all, v_all, sb_bias[j], batch, seq, j)
            os_ = _sb_sample(qkv_s, sb_bias[j], cache_sb_k, cache_sb_v, page_table, dec_b, dec_seq, j)
            sbk_s.append(qkv_s[1].reshape(dec_b, dec_seq, SB_HEADS, dh))
            sbv_s.append(qkv_s[2].reshape(dec_b, dec_seq, SB_HEADS, dh))
        elif kind == 1:
            pr_p, pr_s, wo = _in_proj(st_p, hp, hs, gla_w_in, gla_w_out, j, 3)
            g_p = _gla_gate(st_p, hp, gla_w_in, gla_w_gate, gla_b_gate, j)
            g_s = _gla_gate(st_s, hs, gla_w_in, gla_w_gate, gla_b_gate, j)
            s0 = jnp.zeros((batch,) + state_gla.shape[2:], F32)
            op, sp = _gla(pr_p, g_p, gla_norm_g[j], s0, batch, seq)
            os_, ss = _gla(pr_s, g_s, gla_norm_g[j], state_gla[j], dec_b, dec_seq)
            gla_p.append(sp)
            gla_s.append(ss)
        else:
            pr_p, pr_s, wo = _in_proj(st_p, hp, hs, conv_w_in, conv_w_out, j, 3)
            b0 = jnp.zeros((batch, CONV_WIDTH - 1, d), F32)
            op, bp = _conv(pr_p, conv_w[j], b0, batch, seq)
            os_, bs = _conv(pr_s, conv_w[j], state_conv[j], dec_b, dec_seq)
            conv_p.append(bp)
            conv_s.append(bs)
        xp, xs, hp, hs = _out_norm(st_p, st_s, op, os_, wo, xp, xs, l, 1, 1.0, norm_g[l, 2], l, 2)
        if l + 1 < depth:
            xp, xs, hp, hs = ffn_half(xp, xs, hp, hs, l, 2, 1, norm_g[l + 1, 0], l + 1, 0)
        else:
            xp, xs, hp, hs = ffn_half(xp, xs, hp, hs, l, 2, 1, final_norm_g, None, None)

    y_prompt = hp.reshape(batch, seq, d)
    y_sample = hs.reshape(dec_b, dec_seq, d)
    kv_shape = (n_sb, batch, seq, SB_HEADS, dh)
    return (y_prompt, y_sample, k_all.reshape(kv_shape), v_all.reshape(kv_shape), jnp.stack(sbk_s), jnp.stack(sbv_s),
            jnp.stack(gla_p), jnp.stack(gla_s), jnp.stack(conv_p), jnp.stack(conv_s))
```
